```python
import jax
import jax.numpy as jnp
from jax import lax
import numpy as np

D_MODEL = 1024
BATCH = 8
SEQ = 2048
DEPTH = 2
DEC_BATCH = 32
DEC_SEQ = 8
PAST_LEN = 8192
PAGE_SIZE = 128

A_HEADS = 8
A_DK = D_MODEL // A_HEADS
A_DV = D_MODEL // A_HEADS
A_CHUNK = 32
B_HEADS = 16
B_DH = D_MODEL // B_HEADS
Q_BLOCK = 128
N_EXPERTS = 32
TOP_K = 4
D_FF = D_MODEL
SWIGLU_LIMIT = 7.0
SWIGLU_ALPHA = 1.702
DN_ALPHA = (2 * DEPTH) ** 0.25
DN_BETA = (8 * DEPTH) ** -0.25
LN_EPS = 1e-5
RMS_EPS = 1e-6
MASK_VALUE = -1e30

A_QK = A_HEADS * A_DK
A_V = A_HEADS * A_DV
B_W = B_HEADS * B_DH
SPLITS = (A_QK, 2 * A_QK, 2 * A_QK + A_V, 2 * A_QK + 2 * A_V,
          2 * A_QK + 2 * A_V + B_W, 2 * A_QK + 2 * A_V + 2 * B_W, 2 * A_QK + 2 * A_V + 3 * B_W,
          2 * A_QK + 2 * A_V + 3 * B_W + B_HEADS, 2 * A_QK + 2 * A_V + 3 * B_W + B_HEADS + D_MODEL)
N_IN = 2 * A_QK + 2 * A_V + 3 * B_W + B_HEADS + 2 * D_MODEL

kernel_name = 'hgrn2_fox_gated_deepnorm_moe_step'


def layer_norm(x, g, b):
    xf = x.astype(jnp.float32)
    mu = jnp.mean(xf, axis=-1, keepdims=True)
    var = jnp.mean(jnp.square(xf - mu), axis=-1, keepdims=True)
    return ((xf - mu) * lax.rsqrt(var + LN_EPS) * g + b).astype(x.dtype)


def rms_norm(x, g):
    xf = x.astype(jnp.float32)
    return (xf * lax.rsqrt(jnp.mean(jnp.square(xf), axis=-1, keepdims=True) + RMS_EPS) * g).astype(x.dtype)


def gather_pages(pool, page_table):
    rows = pool[page_table]
    return rows.reshape((rows.shape[0], rows.shape[1] * rows.shape[2]) + rows.shape[3:])


def hgrn2_chunked(q, k, v, logf, s0):
    nb, L, H, _ = q.shape
    pad = (-L) % A_CHUNK
    n = (L + pad) // A_CHUNK

    def blocks(t):
        t = jnp.pad(t.astype(jnp.float32), ((0, 0), (0, pad), (0, 0), (0, 0)))
        return t.reshape(nb, n, A_CHUNK, H, t.shape[-1]).transpose(1, 0, 3, 2, 4)

    causal = jnp.tril(jnp.ones((A_CHUNK, A_CHUNK), dtype=bool))[:, :, None]

    def step(S, xs):
        qi, ki, vi, gi = xs
        G = jnp.cumsum(gi, axis=2)
        diff = G[:, :, :, None, :] - G[:, :, None, :, :]
        decay = jnp.where(causal, jnp.exp(jnp.where(causal, diff, 0.0)), 0.0)
        att = jnp.einsum('bhtk,bhsk,bhtsk->bhts', qi, ki, decay)
        o = jnp.einsum('bhts,bhsv->bhtv', att, vi) + jnp.einsum('bhtk,bhkv->bhtv', qi * jnp.exp(G), S)
        g_end = G[:, :, -1, :]
        S = jnp.exp(g_end)[..., None] * S + jnp.einsum('bhsk,bhsv->bhkv', ki * jnp.exp(g_end[:, :, None, :] - G), vi)
        return S, o

    S, o = lax.scan(step, s0.astype(jnp.float32), (blocks(q), blocks(k), blocks(v), blocks(logf)))
    o = o.transpose(1, 0, 3, 2, 4).reshape(nb, n * A_CHUNK, H, v.shape[-1])[:, :L]
    return o.astype(q.dtype), S.astype(s0.dtype)


def fox_attention(q, k, v, cum_q, cum_k, q_start):
    nb, Lq, H, Dh = q.shape
    Lk = k.shape[1]
    qb = Q_BLOCK if Lq % Q_BLOCK == 0 else Lq
    n = Lq // qb
    q_blocks = q.reshape(nb, n, qb, H, Dh).transpose(1, 0, 3, 2, 4)
    f_blocks = cum_q.reshape(nb, n, qb, H).transpose(1, 0, 3, 2)
    q_pos = (q_start + jnp.arange(Lq)).reshape(n, qb)
    k_pos = jnp.arange(Lk)
    fk = cum_k.transpose(0, 2, 1)
    scale = Dh ** -0.5

    def block(args):
        qi, fi, pi = args
        s = jnp.einsum('bhqd,bkhd->bhqk', qi, k, preferred_element_type=jnp.float32) * scale
        s = s + fi[..., None] - fk[:, :, None, :]
        s = jnp.where(pi[:, None] >= k_pos[None, :], s, MASK_VALUE)
        p = jax.nn.softmax(s, axis=-1).astype(v.dtype)
        return jnp.einsum('bhqk,bkhd->bqhd', p, v)

    o = lax.map(block, (q_blocks, f_blocks, q_pos))
    return o.transpose(1, 0, 2, 3, 4).reshape(nb, Lq, H * Dh)


def moe(x, w_router, b_router, w_gate_up, b_gate_up, w_down, b_down):
    nb, L, D = x.shape
    t = x.reshape(nb * L, D)
    logits = (t @ w_router + b_router).astype(jnp.float32)
    top_val, top_idx = lax.top_k(logits, TOP_K)
    gates = jax.nn.softmax(top_val, axis=-1)
    comb = jnp.einsum('tk,tke->te', gates, jax.nn.one_hot(top_idx, N_EXPERTS, dtype=jnp.float32)).astype(x.dtype)
    out = jnp.zeros_like(t)
    for e in range(N_EXPERTS):
        h = t @ w_gate_up[e] + b_gate_up[e]
        gate, up = jnp.split(h, 2, axis=-1)
        gate = jnp.minimum(gate, SWIGLU_LIMIT)
        up = jnp.clip(up, -SWIGLU_LIMIT, SWIGLU_LIMIT)
        act = (up + 1) * gate * jax.nn.sigmoid(SWIGLU_ALPHA * gate)
        out = out + comb[:, e:e + 1] * (act @ w_down[e] + b_down[e])
    return out.reshape(nb, L, D)


def trunk_layer(x, s0, k_past, v_past, logf_past, lb, params):
    (w_in, b_fox_f, hgrn_norm_g, w_branch_a, w_branch_b, w_out, ln_mix_g, ln_mix_b,
     w_router, b_router, w_gate_up, b_gate_up, w_down, b_down, ln_ffn_g, ln_ffn_b) = params
    nb, L, _ = x.shape
    proj = x @ w_in
    a_q, a_f, a_i, a_g, b_q, b_k, b_v, b_f, g_a, g_b = jnp.split(proj, SPLITS, axis=-1)

    q_a = jax.nn.silu(a_q).reshape(nb, L, A_HEADS, A_DK) * (A_DK ** -0.5)
    lb_h = lb.reshape(A_HEADS, A_DK)
    z = a_f.astype(jnp.float32).reshape(nb, L, A_HEADS, A_DK)
    f_a = lb_h + (1.0 - lb_h) * jax.nn.sigmoid(z)
    logf_a = jnp.log(f_a)
    k_a = 1.0 - f_a
    v_a = a_i.reshape(nb, L, A_HEADS, A_DV)
    o_a, s_new = hgrn2_chunked(q_a, k_a, v_a, logf_a, s0)
    o_a = rms_norm(o_a, hgrn_norm_g) * jax.nn.silu(a_g.reshape(nb, L, A_HEADS, A_DV))
    o_a = o_a.reshape(nb, L, A_V)

    q_b = b_q.reshape(nb, L, B_HEADS, B_DH)
    k_b = b_k.reshape(nb, L, B_HEADS, B_DH)
    v_b = b_v.reshape(nb, L, B_HEADS, B_DH)
    logf_b = jax.nn.log_sigmoid((b_f + b_fox_f).astype(jnp.float32))
    past = k_past.shape[1]
    k_all = jnp.concatenate([k_past, k_b.astype(k_past.dtype)], axis=1)
    v_all = jnp.concatenate([v_past, v_b.astype(v_past.dtype)], axis=1)
    cum = jnp.cumsum(jnp.concatenate([logf_past.astype(jnp.float32), logf_b], axis=1), axis=1)
    o_b = fox_attention(q_b, k_all, v_all, cum[:, past:], cum, past)

    mix = (jax.nn.sigmoid(g_a) * (o_a @ w_branch_a) + jax.nn.sigmoid(g_b) * (o_b @ w_branch_b)) @ w_out
    h = layer_norm(DN_ALPHA * x + mix, ln_mix_g, ln_mix_b)
    ffn = moe(h, w_router, b_router, w_gate_up, b_gate_up, w_down, b_down)
    y = layer_norm(DN_ALPHA * h + ffn, ln_ffn_g, ln_ffn_b)
    return y, s_new, k_b, v_b, logf_b.astype(logf_past.dtype)


def setup_inputs(seed: int = 0) -> dict:
    key = jax.random.key(seed)
    ks = jax.random.split(key, 24)
    nrm = jax.random.normal
    n_pages = PAST_LEN // PAGE_SIZE
    n_used = DEC_BATCH * n_pages
    n_pool = n_used + max(1, n_used // 4)
    page_table = jax.random.permutation(ks[6], n_pool)[:n_used].reshape(DEC_BATCH, n_pages).astype(jnp.int32)
    return {
        'x_prompt': nrm(ks[0], (BATCH, SEQ, D_MODEL), jnp.float32),
        'x_sample': nrm(ks[1], (DEC_BATCH, DEC_SEQ, D_MODEL), jnp.float32),
        'cache_k': nrm(ks[2], (DEPTH, n_pool, PAGE_SIZE, B_HEADS, B_DH), jnp.float32),
        'cache_v': nrm(ks[3], (DEPTH, n_pool, PAGE_SIZE, B_HEADS, B_DH), jnp.float32),
        'cache_logf': jax.nn.log_sigmoid(2.0 + nrm(ks[4], (DEPTH, n_pool, PAGE_SIZE, B_HEADS), jnp.float32)),
        'state_hgrn': 0.5 * nrm(ks[5], (DEPTH, DEC_BATCH, A_HEADS, A_DK, A_DV), jnp.float32),
        'page_table': page_table,
        'w_in': nrm(ks[7], (DEPTH, D_MODEL, N_IN), jnp.float32) * D_MODEL ** -0.5,
        'b_fox_f': 0.1 * nrm(ks[8], (DEPTH, B_HEADS), jnp.float32),
        'lb_logits': 0.5 * nrm(ks[9], (DEPTH, A_QK), jnp.float32),
        'hgrn_norm_g': 1.0 + 0.02 * nrm(ks[10], (DEPTH, A_DV), jnp.float32),
        'w_branch_a': nrm(ks[11], (DEPTH, A_V, D_MODEL), jnp.float32) * A_V ** -0.5,
        'w_branch_b': nrm(ks[12], (DEPTH, B_W, D_MODEL), jnp.float32) * B_W ** -0.5,
        'w_out': nrm(ks[13], (DEPTH, D_MODEL, D_MODEL), jnp.float32) * (D_MODEL ** -0.5 * DN_BETA),
        'ln_mix_g': 1.0 + 0.02 * nrm(ks[14], (DEPTH, D_MODEL), jnp.float32),
        'ln_mix_b': 0.02 * nrm(ks[15], (DEPTH, D_MODEL), jnp.float32),
        'w_router': nrm(ks[16], (DEPTH, D_MODEL, N_EXPERTS), jnp.float32) * D_MODEL ** -0.5,
        'b_router': 0.01 * nrm(ks[17], (DEPTH, N_EXPERTS), jnp.float32),
        'w_gate_up': nrm(ks[18], (DEPTH, N_EXPERTS, D_MODEL, 2 * D_FF), jnp.float32) * D_MODEL ** -0.5,
        'b_gate_up': 0.01 * nrm(ks[19], (DEPTH, N_EXPERTS, 2 * D_FF), jnp.float32),
        'w_down': nrm(ks[20], (DEPTH, N_EXPERTS, D_FF, D_MODEL), jnp.float32) * (D_FF ** -0.5 * DN_BETA),
        'b_down': 0.01 * nrm(ks[21], (DEPTH, N_EXPERTS, D_MODEL), jnp.float32),
        'ln_ffn_g': 1.0 + 0.02 * nrm(ks[22], (DEPTH, D_MODEL), jnp.float32),
        'ln_ffn_b': 0.02 * nrm(ks[23], (DEPTH, D_MODEL), jnp.float32),
    }


def reference(x_prompt, x_sample, cache_k, cache_v, cache_logf, state_hgrn, page_table,
              w_in, b_fox_f, lb_logits, hgrn_norm_g, w_branch_a, w_branch_b, w_out, ln_mix_g, ln_mix_b,
              w_router, b_router, w_gate_up, b_gate_up, w_down, b_down, ln_ffn_g, ln_ffn_b):
    pl = jax.nn.softmax(lb_logits.astype(jnp.float32), axis=0)
    lower_bounds = jnp.cumsum(pl, axis=0) - pl[0:1]

    y_p, y_s = x_prompt, x_sample
    zero_state = jnp.zeros((BATCH, A_HEADS, A_DK, A_DV), state_hgrn.dtype)
    empty_k = jnp.zeros((BATCH, 0, B_HEADS, B_DH), cache_k.dtype)
    empty_v = jnp.zeros((BATCH, 0, B_HEADS, B_DH), cache_v.dtype)
    empty_logf = jnp.zeros((BATCH, 0, B_HEADS), cache_logf.dtype)
    kp, vp, fp, sp, ks_, vs_, fs_, ss_ = [], [], [], [], [], [], [], []
    for l in range(DEPTH):
        params = (w_in[l], b_fox_f[l], hgrn_norm_g[l], w_branch_a[l], w_branch_b[l], w_out[l],
                  ln_mix_g[l], ln_mix_b[l], w_router[l], b_router[l], w_gate_up[l], b_gate_up[l],
                  w_down[l], b_down[l], ln_ffn_g[l], ln_ffn_b[l])
        y_p, s_p, k_p, v_p, f_p = trunk_layer(y_p, zero_state, empty_k, empty_v, empty_logf,
                                              lower_bounds[l], params)
        y_s, s_s, k_s, v_s, f_s = trunk_layer(y_s, state_hgrn[l],
                                              gather_pages(cache_k[l], page_table),
                                              gather_pages(cache_v[l], page_table),
                                              gather_pages(cache_logf[l], page_table),
                                              lower_bounds[l], params)
        kp.append(k_p); vp.append(v_p); fp.append(f_p); sp.append(s_p)
        ks_.append(k_s); vs_.append(v_s); fs_.append(f_s); ss_.append(s_s)
    k_new_prompt = jnp.stack(kp)
    v_new_prompt = jnp.stack(vp)
    logf_new_prompt = jnp.stack(fp)
    state_hgrn_prompt = jnp.stack(sp)
    k_new_sample = jnp.stack(ks_)
    v_new_sample = jnp.stack(vs_)
    logf_new_sample = jnp.stack(fs_)
    state_hgrn_sample = jnp.stack(ss_)
    return (y_p, y_s, k_new_prompt, v_new_prompt, logf_new_prompt, state_hgrn_prompt,
            k_new_sample, v_new_sample, logf_new_sample, state_hgrn_sample)
```

```python
import functools

import jax
import jax.numpy as jnp
from jax import lax
from jax.experimental import pallas as pl
from jax.experimental.pallas import tpu as pltpu

F32 = jnp.float32
BF16 = jnp.bfloat16

D_MODEL = 1024
A_HEADS = 8
A_DK = 128
B_HEADS = 16
B_DH = 64
N_EXPERTS = 32
TOP_K = 4
SWIGLU_LIMIT = 7.0
SWIGLU_ALPHA = 1.702
LN_EPS = 1e-5
RMS_EPS = 1e-6
MASK_VALUE = -1e30
LANES = 128
HGRN_CHUNK = 32
HGRN_SAFE_DECAY = 60.0
VMEM_LIMIT = 56 * 1024 * 1024
PROJ_Q0 = 4 * D_MODEL
PROJ_G0 = 5 * D_MODEL
MOE_TILE = 256


def _pick_tile(n, candidates):
    for c in candidates:
        if n % c == 0:
            return c
    return n


def _cparams(sem, vmem=None):
    return pltpu.CompilerParams(dimension_semantics=sem, vmem_limit_bytes=vmem or VMEM_LIMIT)


def _split3(x):
    hi = x.astype(BF16)
    r = x - hi.astype(F32)
    mid = r.astype(BF16)
    lo = (r - mid.astype(F32)).astype(BF16)
    return hi, mid, lo


def _dot(a, b):
    return jnp.dot(a, b, preferred_element_type=F32)


def _dot_nt(a, b):
    return lax.dot_general(a, b, (((1,), (1,)), ((), ())), preferred_element_type=F32)


def _dot_tn(a, b):
    return lax.dot_general(a, b, (((0,), (0,)), ((), ())), preferred_element_type=F32)


def _tri(n, upper=False):
    r = lax.broadcasted_iota(jnp.int32, (n, n), 0)
    c = lax.broadcasted_iota(jnp.int32, (n, n), 1)
    return ((r <= c) if upper else (c <= r)).astype(BF16)


def _cumsum_rows(x):
    tri = _tri(x.shape[0])
    hi, mid, lo = _split3(x)
    return _dot(tri, hi) + _dot(tri, mid) + _dot(tri, lo)


def _cumsum_lanes(x):
    tri = _tri(x.shape[1], upper=True)
    hi, mid, lo = _split3(x)
    return _dot(hi, tri) + _dot(mid, tri) + _dot(lo, tri)


def _log_sigmoid(z):
    return jnp.minimum(z, 0.0) - jnp.log(1.0 + jnp.exp(-jnp.abs(z)))


def _sigmoid(z):
    return 1.0 / (1.0 + jnp.exp(-z))


def _silu(z):
    return z * _sigmoid(z)


def _layer_norm(x, g, b):
    mu = jnp.mean(x, axis=-1, keepdims=True)
    xc = x - mu
    var = jnp.mean(xc * xc, axis=-1, keepdims=True)
    return xc * lax.rsqrt(var + LN_EPS) * g + b


def _drop_ref(kern, idx):
    def wrapped(*refs):
        return kern(*(refs[:idx] + refs[idx + 1:]))
    return wrapped


def _proj_kernel(x_ref, wt_ref, o_ref):
    o_ref[...] = _dot_nt(x_ref[...].astype(BF16), wt_ref[...])


def _project(x2d, wt, row0=0, rows=None):
    d = x2d.shape[1]
    t = x2d.shape[0] if rows is None else rows
    n = wt.shape[0]
    tm = _pick_tile(t, (1280, 640, 512, 256))
    tn = _pick_tile(n, (1024, 512, 256, 128))
    rb0 = row0 // tm
    return pl.pallas_call(
        _proj_kernel,
        grid=(n // tn, t // tm),
        in_specs=[pl.BlockSpec((tm, d), lambda j, i: (rb0 + i, 0)),
                  pl.BlockSpec((tn, d), lambda j, i: (j, 0))],
        out_specs=pl.BlockSpec((tm, tn), lambda j, i: (i, j)),
        out_shape=jax.ShapeDtypeStruct((t, n), F32),
        compiler_params=_cparams(("parallel", "parallel")),
        name="in_proj",
    )(x2d, wt)


def _kv_prompt_kernel(x_ref, wt_ref, k_ref, v_ref, *, layer, fill_others):
    def compute():
        kv = _dot_nt(wt_ref[...], x_ref[...].astype(BF16))
        k_ref[0, 0] = kv[:D_MODEL]
        v_ref[0, 0] = kv[D_MODEL:]

    if not fill_others:
        compute()
        return
    s = pl.program_id(0)
    pl.when(s == layer)(compute)

    @pl.when(s != layer)
    def _():
        k_ref[...] = jnp.zeros_like(k_ref)
        v_ref[...] = jnp.zeros_like(v_ref)


def _kv_prompt(x2d, wt_kv, layer, depth, nb, seq, prev):
    d = x2d.shape[1]
    tl = _pick_tile(seq, (512, 256, 128))
    nl = seq // tl
    first = prev is None
    in_specs = [pl.BlockSpec((tl, d), lambda s, b, i: (b * nl + i, 0)),
                pl.BlockSpec((2 * D_MODEL, d), lambda s, b, i: (0, 0))]
    args = [x2d, wt_kv]
    kern = functools.partial(_kv_prompt_kernel, layer=layer, fill_others=first)
    aliases = {}
    if first:
        ospec = pl.BlockSpec((1, 1, D_MODEL, tl), lambda s, b, i: (s, b, 0, i))
    else:
        in_specs += [pl.BlockSpec(memory_space=pl.ANY)] * 2
        args += list(prev)
        aliases = {2: 0, 3: 1}
        kern = _drop_ref(_drop_ref(kern, 2), 3)
        ospec = pl.BlockSpec((1, 1, D_MODEL, tl), lambda s, b, i: (layer, b, 0, i))
    out = jax.ShapeDtypeStruct((depth, nb, D_MODEL, seq), F32)
    return pl.pallas_call(
        kern,
        grid=(depth if first else 1, nb, nl),
        in_specs=in_specs,
        out_specs=[ospec, ospec],
        out_shape=[out, out],
        input_output_aliases=aliases,
        compiler_params=_cparams(("parallel", "parallel", "parallel")),
        name="kv_prompt",
    )(*args)


def _fox_gate_prompt_kernel(x_ref, wft_ref, bf_ref, bft_ref, logft_ref, cum_ref, cumt_ref, carry_ref, carryt_ref):
    i = pl.program_id(1)

    @pl.when(i == 0)
    def _():
        carry_ref[...] = jnp.zeros_like(carry_ref)
        carryt_ref[...] = jnp.zeros_like(carryt_ref)

    xb = x_ref[...].astype(BF16)
    logf = _log_sigmoid(_dot_nt(xb, wft_ref[...]) + bf_ref[...])
    logft = _log_sigmoid(_dot_nt(wft_ref[...], xb) + bft_ref[...])
    tl = logf.shape[0]
    cs = _cumsum_rows(logf) + carry_ref[...]
    cst = _cumsum_lanes(logft) + carryt_ref[...]
    carry_ref[...] = cs[tl - 1:tl, :]
    carryt_ref[...] = cst[:, tl - 1:tl]
    logft_ref[0] = logft
    cum_ref[0] = cs
    cumt_ref[0] = cst


def _fox_gate_prompt(x2d, wft, bf, nb, seq):
    d = x2d.shape[1]
    tl = _pick_tile(seq, (256, 128))
    nl = seq // tl
    out_t = jax.ShapeDtypeStruct((nb, B_HEADS, seq), F32)
    tspec = pl.BlockSpec((1, B_HEADS, tl), lambda b, i: (b, 0, i))
    return pl.pallas_call(
        _fox_gate_prompt_kernel,
        grid=(nb, nl),
        in_specs=[pl.BlockSpec((tl, d), lambda b, i: (b * nl + i, 0)),
                  pl.BlockSpec((B_HEADS, d), lambda b, i: (0, 0)),
                  pl.BlockSpec((1, B_HEADS), lambda b, i: (0, 0)),
                  pl.BlockSpec((B_HEADS, 1), lambda b, i: (0, 0))],
        out_specs=[tspec, pl.BlockSpec((1, tl, B_HEADS), lambda b, i: (b, i, 0)), tspec],
        out_shape=[out_t, jax.ShapeDtypeStruct((nb, seq, B_HEADS), F32), out_t],
        scratch_shapes=[pltpu.VMEM((1, B_HEADS), F32), pltpu.VMEM((B_HEADS, 1), F32)],
        compiler_params=_cparams(("parallel", "arbitrary")),
        name="fox_gate_prompt",
    )(x2d, wft, bf.reshape(1, B_HEADS), bf.reshape(B_HEADS, 1))


def _fox_gate_sample_kernel(pt_ref, x_ref, wft_ref, bft_ref, clf_ref, cumpast_ref, logft_ref, cumnew_ref,
                            carryt_ref, *, n_pages):
    del pt_ref
    p = pl.program_id(1)

    @pl.when(p == 0)
    def _():
        carryt_ref[...] = jnp.zeros_like(carryt_ref)

    @pl.when(p < n_pages)
    def _():
        cst = _cumsum_lanes(clf_ref[0]) + carryt_ref[...]
        carryt_ref[...] = cst[:, cst.shape[1] - 1:]
        cumpast_ref[0] = cst

    @pl.when(p == n_pages)
    def _():
        logft = _log_sigmoid(_dot_nt(wft_ref[...], x_ref[...].astype(BF16)) + bft_ref[...])
        logft_ref[0] = logft
        cumnew_ref[0] = _cumsum_lanes(logft) + carryt_ref[...]


def _fox_gate_sample(x2d, wft, bf, cache_logf_t, page_table, page0, row0, nb, ns):
    d = x2d.shape[1]
    n_pages = page_table.shape[1]
    page = cache_logf_t.shape[2]
    rb0 = row0 // ns
    new_spec = pl.BlockSpec((1, B_HEADS, ns), lambda b, p, pt: (b, 0, 0))
    grid_spec = pltpu.PrefetchScalarGridSpec(
        num_scalar_prefetch=1,
        grid=(nb, n_pages + 1),
        in_specs=[pl.BlockSpec((ns, d), lambda b, p, pt: (rb0 + b, 0)),
                  pl.BlockSpec((B_HEADS, d), lambda b, p, pt: (0, 0)),
                  pl.BlockSpec((B_HEADS, 1), lambda b, p, pt: (0, 0)),
                  pl.BlockSpec((1, B_HEADS, page),
                               lambda b, p, pt: (page0 + pt[b, jnp.minimum(p, n_pages - 1)], 0, 0))],
        out_specs=[pl.BlockSpec((1, B_HEADS, page), lambda b, p, pt: (b, 0, jnp.minimum(p, n_pages - 1))),
                   new_spec, new_spec],
        scratch_shapes=[pltpu.VMEM((B_HEADS, 1), F32)],
    )
    return pl.pallas_call(
        functools.partial(_fox_gate_sample_kernel, n_pages=n_pages),
        grid_spec=grid_spec,
        out_shape=[jax.ShapeDtypeStruct((nb, B_HEADS, n_pages * page), F32),
                   jax.ShapeDtypeStruct((nb, B_HEADS, ns), F32),
                   jax.ShapeDtypeStruct((nb, B_HEADS, ns), F32)],
        compiler_params=_cparams(("parallel", "arbitrary")),
        name="fox_gate_sample",
    )(page_table, x2d, wft, bf.reshape(B_HEADS, 1), cache_logf_t)


def _hgrn_kernel(*refs, chunk, n_chunks, has_s0):
    if has_s0:
        (aq_ref, af_ref, ai_ref, ag_ref, lb_ref, g_ref, s0_ref, o_ref, snew_ref,
         st_ref, a_ref, gs_ref, ks_ref, qs_ref) = refs
    else:
        (aq_ref, af_ref, ai_ref, ag_ref, lb_ref, g_ref, o_ref, snew_ref,
         st_ref, a_ref, gs_ref, ks_ref, qs_ref) = refs
        s0_ref = None
    i = pl.program_id(1)
    c = chunk

    @pl.when(i == 0)
    def _():
        for h in range(A_HEADS):
            if has_s0:
                st_ref[h] = s0_ref[0, h].T
            else:
                st_ref[h] = jnp.zeros((A_DK, A_DK), F32)

    lbv = lb_ref[...]
    gv = g_ref[...]
    row = lax.broadcasted_iota(jnp.int32, (c, c), 0)
    col = lax.broadcasted_iota(jnp.int32, (c, c), 1)
    causal = col <= row

    def do_chunk(ci, carry):
        r = pl.ds(pl.multiple_of(ci * c, c), c)
        z = af_ref[r, :]
        f = lbv + (1.0 - lbv) * _sigmoid(z)
        logf = jnp.log(f)
        kk = (1.0 - lbv) * _sigmoid(-z)
        g_cum = _cumsum_rows(logf)
        gend = g_cum[c - 1:c, :]
        q = _silu(aq_ref[r, :]) * (A_DK ** -0.5)
        vb = ai_ref[r, :].astype(BF16)
        qt = (q * jnp.exp(g_cum)).astype(BF16)
        khat = (kk * jnp.exp(gend - g_cum)).astype(BF16)
        safe = jnp.min(gend) >= -HGRN_SAFE_DECAY

        @pl.when(safe)
        def _():
            kt = (kk * jnp.exp(-g_cum)).astype(BF16)
            for h in range(A_HEADS):
                hs = slice(h * A_DK, (h + 1) * A_DK)
                a_ref[h] = _dot_nt(qt[:, hs], kt[:, hs])

        @pl.when(jnp.logical_not(safe))
        def _():
            gs_ref[...] = g_cum
            ks_ref[...] = kk
            qs_ref[...] = q
            a_ref[...] = jnp.zeros_like(a_ref)

            def col_j(j, carry2):
                gj = gs_ref[pl.ds(j, 1), :]
                kj = ks_ref[pl.ds(j, 1), :]
                w = qs_ref[...] * kj * jnp.exp(jnp.minimum(gs_ref[...] - gj, 0.0))
                for h in range(A_HEADS):
                    hs = slice(h * A_DK, (h + 1) * A_DK)
                    cj = jnp.sum(w[:, hs], axis=1, keepdims=True)
                    a_ref[h] = a_ref[h] + jnp.where(col == j, cj, 0.0)
                return carry2

            lax.fori_loop(0, c, col_j, 0)

        egend = jnp.exp(gend)
        for h in range(A_HEADS):
            hs = slice(h * A_DK, (h + 1) * A_DK)
            att = jnp.where(causal, a_ref[h], 0.0).astype(BF16)
            st = st_ref[h]
            o = _dot(att, vb[:, hs]) + _dot_nt(qt[:, hs], st.astype(BF16))
            st_ref[h] = st * egend[:, hs] + _dot_tn(vb[:, hs], khat[:, hs])
            ms = jnp.mean(o * o, axis=-1, keepdims=True)
            o = o * lax.rsqrt(ms + RMS_EPS) * gv
            o_ref[r, hs] = o * _silu(ag_ref[r, hs])
        return carry

    lax.fori_loop(0, n_chunks, do_chunk, 0)

    @pl.when(i == pl.num_programs(1) - 1)
    def _():
        for h in range(A_HEADS):
            snew_ref[0, h] = st_ref[h].T


def _hgrn(proj, lb, g, s0, s0_base, row0, nb, seq):
    chunk = min(HGRN_CHUNK, seq)
    lb_rows = _pick_tile(seq, (256, chunk))
    nl = seq // lb_rows
    rb0 = row0 // lb_rows
    has_s0 = s0 is not None
    in_specs = [pl.BlockSpec((lb_rows, D_MODEL), functools.partial(lambda b, i, k: (rb0 + b * nl + i, k), k=k))
                for k in range(4)]
    in_specs += [pl.BlockSpec((1, D_MODEL), lambda b, i: (0, 0)),
                 pl.BlockSpec((1, A_DK), lambda b, i: (0, 0))]
    args = [proj, proj, proj, proj, lb, g]
    if has_s0:
        in_specs.append(pl.BlockSpec((1, A_HEADS, A_DK, A_DK), lambda b, i: (s0_base + b, 0, 0, 0)))
        args.append(s0)
    kern = functools.partial(_hgrn_kernel, chunk=chunk, n_chunks=lb_rows // chunk, has_s0=has_s0)
    return pl.pallas_call(
        kern,
        grid=(nb, nl),
        in_specs=in_specs,
        out_specs=[pl.BlockSpec((lb_rows, D_MODEL), lambda b, i: (b * nl + i, 0)),
                   pl.BlockSpec((1, A_HEADS, A_DK, A_DK), lambda b, i: (b, 0, 0, 0))],
        out_shape=[jax.ShapeDtypeStruct((nb * seq, D_MODEL), F32),
                   jax.ShapeDtypeStruct((nb, A_HEADS, A_DK, A_DK), F32)],
        scratch_shapes=[pltpu.VMEM((A_HEADS, A_DK, A_DK), F32),
                        pltpu.VMEM((A_HEADS, chunk, chunk), F32),
                        pltpu.VMEM((chunk, D_MODEL), F32),
                        pltpu.VMEM((chunk, D_MODEL), F32),
                        pltpu.VMEM((chunk, D_MODEL), F32)],
        compiler_params=_cparams(("parallel", "arbitrary")),
        name="hgrn",
    )(*args)


def _fox_prompt_kernel(q_ref, k_ref, v_ref, cum_ref, cumt_ref, o_ref, m_ref, l_ref, acc_ref, *, tq):
    j = pl.program_id(1)
    seq = q_ref.shape[0]
    nq = seq // tq
    lane = lax.broadcasted_iota(jnp.int32, (tq, LANES), 1)
    lo = lane < B_DH
    rows = lax.broadcasted_iota(jnp.int32, (2 * tq, tq), 0)
    cols = lax.broadcasted_iota(jnp.int32, (2 * tq, tq), 1)
    qpos = jnp.where(rows >= tq, rows - tq, rows)
    hsel = lax.broadcasted_iota(jnp.int32, (tq, B_HEADS), 1)
    hrow = lax.broadcasted_iota(jnp.int32, (B_HEADS, tq), 0)

    def q_block(qi, carry):
        qr = pl.ds(pl.multiple_of(qi * tq, tq), tq)
        q = q_ref[qr, :] * (B_DH ** -0.5)
        q2 = jnp.concatenate([jnp.where(lo, q, 0.0), jnp.where(lo, 0.0, q)], axis=0).astype(BF16)
        cq = cum_ref[0, qr, :]
        fq0 = jnp.sum(jnp.where(hsel == 2 * j, cq, 0.0), axis=1, keepdims=True)
        fq1 = jnp.sum(jnp.where(hsel == 2 * j + 1, cq, 0.0), axis=1, keepdims=True)
        fq = jnp.concatenate([fq0, fq1], axis=0)
        m_ref[...] = jnp.full_like(m_ref, -jnp.inf)
        l_ref[...] = jnp.zeros_like(l_ref)
        acc_ref[...] = jnp.zeros_like(acc_ref)

        def kv_block(ki, carry2):
            kr = pl.ds(pl.multiple_of(ki * tq, tq), tq)
            kb = k_ref[0, 0, :, kr].astype(BF16)
            vb = v_ref[0, 0, :, kr].astype(BF16)
            ck = cumt_ref[0, :, kr]
            fk0 = jnp.sum(jnp.where(hrow == 2 * j, ck, 0.0), axis=0, keepdims=True)
            fk1 = jnp.sum(jnp.where(hrow == 2 * j + 1, ck, 0.0), axis=0, keepdims=True)
            s = _dot(q2, kb) + fq
            s = s - jnp.where(rows >= tq, fk1, fk0)
            s = jnp.where(qi * tq + qpos >= ki * tq + cols, s, MASK_VALUE)
            m_old = m_ref[...]
            m_new = jnp.maximum(m_old, jnp.max(s, axis=1, keepdims=True))
            alpha = jnp.exp(m_old - m_new)
            p = jnp.exp(s - m_new)
            l_ref[...] = alpha * l_ref[...] + jnp.sum(p, axis=1, keepdims=True)
            acc_ref[...] = alpha * acc_ref[...] + _dot_nt(p.astype(BF16), vb)
            m_ref[...] = m_new
            return carry2

        lax.fori_loop(0, qi + 1, kv_block, 0)
        o2 = acc_ref[...] / l_ref[...]
        o_ref[qr, :] = jnp.where(lo, o2[:tq], o2[tq:])
        return carry

    lax.fori_loop(0, nq, q_block, 0)


def _fox_prompt(proj, kbuf, vbuf, layer, cum, cum_t, nb, seq):
    t = nb * seq
    tq = _pick_tile(seq, (256, 128))
    npair = B_HEADS // 2
    c0 = PROJ_Q0 // LANES
    kvspec = pl.BlockSpec((1, 1, LANES, seq), lambda b, j: (layer, b, j, 0))
    return pl.pallas_call(
        functools.partial(_fox_prompt_kernel, tq=tq),
        grid=(nb, npair),
        in_specs=[pl.BlockSpec((seq, LANES), lambda b, j: (b, c0 + j)), kvspec, kvspec,
                  pl.BlockSpec((1, seq, B_HEADS), lambda b, j: (b, 0, 0)),
                  pl.BlockSpec((1, B_HEADS, seq), lambda b, j: (b, 0, 0))],
        out_specs=pl.BlockSpec((seq, LANES), lambda b, j: (b, j)),
        out_shape=jax.ShapeDtypeStruct((t, D_MODEL), F32),
        scratch_shapes=[pltpu.VMEM((2 * tq, 1), F32), pltpu.VMEM((2 * tq, 1), F32),
                        pltpu.VMEM((2 * tq, LANES), F32)],
        compiler_params=_cparams(("parallel", "parallel")),
        name="fox_prompt",
    )(proj, kbuf, vbuf, cum, cum_t)


def _fox_sample_kernel(*refs, pages_per_step, ns):
    q_ref, kvn_ref, cumnt_ref, cumt_ref = refs[1:5]
    k_refs = refs[5:5 + pages_per_step]
    v_refs = refs[5 + pages_per_step:5 + 2 * pages_per_step]
    o_ref = refs[5 + 2 * pages_per_step]
    qbd_ref, fq_ref, m_ref, l_ref, acc_ref = refs[6 + 2 * pages_per_step:]
    g = pl.program_id(1)
    nrow = B_HEADS * ns
    rh = lax.broadcasted_iota(jnp.int32, (nrow, D_MODEL), 0) // ns
    ch = lax.broadcasted_iota(jnp.int32, (nrow, D_MODEL), 1) // B_DH

    def expand_rows(x):
        n = x.shape[1]
        return jnp.broadcast_to(x[:, None, :], (B_HEADS, ns, n)).reshape(nrow, n)

    @pl.when(g == 0)
    def _():
        q = q_ref[...] * (B_DH ** -0.5)
        qrep = jnp.broadcast_to(q[None], (B_HEADS, ns, D_MODEL)).reshape(nrow, D_MODEL)
        qbd_ref[...] = jnp.where(rh == ch, qrep, 0.0).astype(BF16)
        tsel = (lax.broadcasted_iota(jnp.int32, (nrow, ns), 0) % ns
                == lax.broadcasted_iota(jnp.int32, (nrow, ns), 1))
        fq_ref[...] = jnp.sum(jnp.where(tsel, expand_rows(cumnt_ref[0]), 0.0), axis=1, keepdims=True)
        m_ref[...] = jnp.full_like(m_ref, -jnp.inf)
        l_ref[...] = jnp.zeros_like(l_ref)
        acc_ref[...] = jnp.zeros_like(acc_ref)

    def update(s, pv):
        m_old = m_ref[...]
        m_new = jnp.maximum(m_old, jnp.max(s, axis=1, keepdims=True))
        alpha = jnp.exp(m_old - m_new)
        p = jnp.exp(s - m_new)
        l_ref[...] = alpha * l_ref[...] + jnp.sum(p, axis=1, keepdims=True)
        acc_ref[...] = alpha * acc_ref[...] + pv(p.astype(BF16))
        m_ref[...] = m_new

    page = k_refs[0].shape[2]
    for pi in range(pages_per_step):
        kt = k_refs[pi][0].astype(BF16)
        vt = v_refs[pi][0].astype(BF16)
        fk = expand_rows(cumt_ref[0, :, pi * page:(pi + 1) * page])
        s = _dot(qbd_ref[...], kt) + fq_ref[...] - fk
        update(s, lambda p, vt=vt: _dot_nt(p, vt))

    @pl.when(g == pl.num_programs(1) - 1)
    def _():
        kb = kvn_ref[:, :D_MODEL].astype(BF16)
        vb = kvn_ref[:, D_MODEL:].astype(BF16)
        fk = expand_rows(cumnt_ref[0])
        s = _dot_nt(qbd_ref[...], kb) + fq_ref[...] - fk
        tq = lax.broadcasted_iota(jnp.int32, (nrow, ns), 0) % ns
        tk = lax.broadcasted_iota(jnp.int32, (nrow, ns), 1)
        s = jnp.where(tq >= tk, s, MASK_VALUE)
        update(s, lambda p: _dot(p, vb))
        o = acc_ref[...] / l_ref[...]
        for h in range(B_HEADS):
            o_ref[:, h * B_DH:(h + 1) * B_DH] = o[h * ns:(h + 1) * ns, h * B_DH:(h + 1) * B_DH]


def _fox_sample(proj, kv_new, cum_new_t, cum_past_t, cache_kt, cache_vt, page_table, page0, row0, nb, ns):
    n_pages = page_table.shape[1]
    page = cache_kt.shape[2]
    pps = _pick_tile(n_pages, (8, 4, 2, 1))
    ng = n_pages // pps
    rb0 = row0 // ns
    nrow = B_HEADS * ns

    def page_spec(pi):
        return pl.BlockSpec((1, D_MODEL, page), lambda b, g, pt: (page0 + pt[b, g * pps + pi], 0, 0))

    in_specs = [pl.BlockSpec((ns, D_MODEL), lambda b, g, pt: (rb0 + b, PROJ_Q0 // D_MODEL)),
                pl.BlockSpec((ns, 2 * D_MODEL), lambda b, g, pt: (b, 0)),
                pl.BlockSpec((1, B_HEADS, ns), lambda b, g, pt: (b, 0, 0)),
                pl.BlockSpec((1, B_HEADS, pps * page), lambda b, g, pt: (b, 0, g))]
    in_specs += [page_spec(pi) for pi in range(pps)] * 2
    args = [proj, kv_new, cum_new_t, cum_past_t] + [cache_kt] * pps + [cache_vt] * pps
    grid_spec = pltpu.PrefetchScalarGridSpec(
        num_scalar_prefetch=1,
        grid=(nb, ng),
        in_specs=in_specs,
        out_specs=pl.BlockSpec((ns, D_MODEL), lambda b, g, pt: (b, 0)),
        scratch_shapes=[pltpu.VMEM((nrow, D_MODEL), BF16), pltpu.VMEM((nrow, 1), F32),
                        pltpu.VMEM((nrow, 1), F32), pltpu.VMEM((nrow, 1), F32),
                        pltpu.VMEM((nrow, D_MODEL), F32)],
    )
    return pl.pallas_call(
        functools.partial(_fox_sample_kernel, pages_per_step=pps, ns=ns),
        grid_spec=grid_spec,
        out_shape=jax.ShapeDtypeStruct((nb * ns, D_MODEL), F32),
        compiler_params=_cparams(("parallel", "arbitrary")),
        name="fox_sample",
    )(page_table, *args)


def _mix_kernel(oap_ref, oas_ref, obp_ref, obs_ref, ga_ref, gb_ref, x_ref, wa_ref, wb_ref, wo_ref, lng_ref,
                lnb_ref, wrh_ref, wrl_ref, br_ref, h_ref, eidx_ref, gate_ref, rank_ref, cnt_ref, carry_ref,
                *, alpha, n_prompt_tiles):
    i = pl.program_id(0)

    @pl.when(i == 0)
    def _():
        carry_ref[...] = jnp.zeros_like(carry_ref)

    is_prompt = i < n_prompt_tiles
    oa = jnp.where(is_prompt, oap_ref[...], oas_ref[...])
    ob = jnp.where(is_prompt, obp_ref[...], obs_ref[...])
    ya = _dot(oa.astype(BF16), wa_ref[...])
    yb = _dot(ob.astype(BF16), wb_ref[...])
    m = _sigmoid(ga_ref[...]) * ya + _sigmoid(gb_ref[...]) * yb
    mix = _dot(m.astype(BF16), wo_ref[...])
    h = _layer_norm(alpha * x_ref[...] + mix, lng_ref[...], lnb_ref[...])
    h_ref[...] = h

    h_hi = h.astype(BF16)
    h_lo = (h - h_hi.astype(F32)).astype(BF16)
    logits = (_dot_nt(h_hi, wrh_ref[...]) + _dot_nt(h_lo, wrh_ref[...]) + _dot_nt(h_hi, wrl_ref[...])
              + br_ref[...])
    tm = logits.shape[0]
    lane = lax.broadcasted_iota(jnp.int32, (tm, LANES), 1)
    work = logits
    idxs, vals = [], []
    for _ in range(TOP_K):
        mx = jnp.max(work, axis=1, keepdims=True)
        ix = jnp.min(jnp.where(work == mx, lane, LANES), axis=1, keepdims=True)
        idxs.append(ix)
        vals.append(mx)
        work = jnp.where(lane == ix, -jnp.inf, work)
    exps = [jnp.exp(v - vals[0]) for v in vals]
    denom = exps[0] + exps[1] + exps[2] + exps[3]
    multihot = jnp.zeros((tm, LANES), F32)
    for ix in idxs:
        multihot = multihot + jnp.where(lane == ix, 1.0, 0.0)
    r = lax.broadcasted_iota(jnp.int32, (tm, tm), 0)
    c = lax.broadcasted_iota(jnp.int32, (tm, tm), 1)
    before = _dot((c < r).astype(BF16), multihot.astype(BF16)) + carry_ref[...]
    carry_ref[...] = carry_ref[...] + jnp.sum(multihot, axis=0, keepdims=True)
    col4 = lax.broadcasted_iota(jnp.int32, (tm, TOP_K), 1)
    eidx = jnp.zeros((tm, TOP_K), jnp.int32)
    gates = jnp.zeros((tm, TOP_K), F32)
    ranks = jnp.zeros((tm, TOP_K), F32)
    for k in range(TOP_K):
        rk = jnp.sum(jnp.where(lane == idxs[k], before, 0.0), axis=1, keepdims=True)
        eidx = jnp.where(col4 == k, idxs[k], eidx)
        gates = jnp.where(col4 == k, exps[k] / denom, gates)
        ranks = jnp.where(col4 == k, rk, ranks)
    eidx_ref[...] = eidx
    gate_ref[...] = gates
    rank_ref[...] = ranks.astype(jnp.int32)
    cnt_ref[...] = carry_ref[...].astype(jnp.int32)


def _mix(oa_p, oa_s, ob_p, ob_s, proj, x2d, wa, wb, wo, lng, lnb, wrh, wrl, br, alpha):
    t = x2d.shape[0]
    t_p, t_s = oa_p.shape[0], oa_s.shape[0]
    tm = next(c for c in (256, 128, 64, 32, 16, 8) if t_p % c == 0 and t_s % c == 0)
    npt = t_p // tm
    row = lambda i: (i, 0)
    prow = lambda i: (jnp.minimum(i, npt - 1), 0)
    srow = lambda i: (jnp.maximum(i - npt, 0), 0)
    const = lambda i: (0, 0)
    wspec = pl.BlockSpec((D_MODEL, D_MODEL), const)
    vspec = pl.BlockSpec((1, D_MODEL), const)
    return pl.pallas_call(
        functools.partial(_mix_kernel, alpha=alpha, n_prompt_tiles=npt),
        grid=(t // tm,),
        in_specs=[pl.BlockSpec((tm, D_MODEL), prow), pl.BlockSpec((tm, D_MODEL), srow),
                  pl.BlockSpec((tm, D_MODEL), prow), pl.BlockSpec((tm, D_MODEL), srow),
                  pl.BlockSpec((tm, D_MODEL), lambda i: (i, PROJ_G0 // D_MODEL)),
                  pl.BlockSpec((tm, D_MODEL), lambda i: (i, PROJ_G0 // D_MODEL + 1)),
                  pl.BlockSpec((tm, D_MODEL), row), wspec, wspec, wspec, vspec, vspec,
                  pl.BlockSpec((LANES, D_MODEL), const), pl.BlockSpec((LANES, D_MODEL), const),
                  pl.BlockSpec((1, LANES), const)],
        out_specs=[pl.BlockSpec((tm, D_MODEL), row), pl.BlockSpec((tm, TOP_K), row),
                   pl.BlockSpec((tm, TOP_K), row), pl.BlockSpec((tm, TOP_K), row),
                   pl.BlockSpec((1, LANES), const)],
        out_shape=[jax.ShapeDtypeStruct((t, D_MODEL), F32), jax.ShapeDtypeStruct((t, TOP_K), jnp.int32),
                   jax.ShapeDtypeStruct((t, TOP_K), F32), jax.ShapeDtypeStruct((t, TOP_K), jnp.int32),
                   jax.ShapeDtypeStruct((1, LANES), jnp.int32)],
        scratch_shapes=[pltpu.VMEM((1, LANES), F32)],
        compiler_params=_cparams(("arbitrary",)),
        name="mix_router",
    )(oa_p, oa_s, ob_p, ob_s, proj, proj, x2d, wa, wb, wo, lng, lnb, wrh, wrl, br)


def _row_copy(src_hbm, dst_vmem, sem, src_row, dst_row):
    return pltpu.make_async_copy(src_hbm.at[pl.ds(src_row, 1), :], dst_vmem.at[pl.ds(dst_row, 1), :], sem)


def _dispatch_kernel(src_ref, nused_ref, h_hbm, o_ref, sem, *, tm):
    i = pl.program_id(0)

    @pl.when(i < nused_ref[0])
    def _():
        def issue(r, carry):
            _row_copy(h_hbm, o_ref, sem, src_ref[i * tm + r], r).start()
            return carry

        lax.fori_loop(0, tm, issue, 0, unroll=8)
        pltpu.make_async_copy(h_hbm.at[pl.ds(0, tm), :], o_ref, sem).wait()

    @pl.when(i >= nused_ref[0])
    def _():
        o_ref[...] = jnp.zeros_like(o_ref)


def _dispatch(h, src_tok, n_used, n_tiles):
    tm = MOE_TILE
    grid_spec = pltpu.PrefetchScalarGridSpec(
        num_scalar_prefetch=2,
        grid=(n_tiles,),
        in_specs=[pl.BlockSpec(memory_space=pl.ANY)],
        out_specs=pl.BlockSpec((tm, D_MODEL), lambda i, s, n: (i, 0)),
        scratch_shapes=[pltpu.SemaphoreType.DMA],
    )
    return pl.pallas_call(
        functools.partial(_dispatch_kernel, tm=tm),
        grid_spec=grid_spec,
        out_shape=jax.ShapeDtypeStruct((n_tiles * tm, D_MODEL), F32),
        compiler_params=_cparams(("arbitrary",)),
        name="moe_dispatch",
    )(src_tok, n_used, h)


def _expert_kernel(te_ref, nused_ref, x_ref, wgu_ref, bgu_ref, wd_ref, bd_ref, g_ref, y_ref, wgu_bf, wd_bf):
    i = pl.program_id(0)
    used = i < nused_ref[0]
    prev = te_ref[jnp.maximum(i - 1, 0)]
    fresh = jnp.logical_or(i == 0, te_ref[i] != prev)

    @pl.when(jnp.logical_and(used, fresh))
    def _():
        wgu_bf[...] = wgu_ref[0].astype(BF16)
        wd_bf[...] = wd_ref[0].astype(BF16)

    @pl.when(used)
    def _():
        x = x_ref[...].astype(BF16)
        hcat = _dot(x, wgu_bf[...]) + bgu_ref[0]
        dff = hcat.shape[1] // 2
        gate = jnp.minimum(hcat[:, :dff], SWIGLU_LIMIT)
        up = jnp.clip(hcat[:, dff:], -SWIGLU_LIMIT, SWIGLU_LIMIT)
        act = (up + 1.0) * gate * _sigmoid(SWIGLU_ALPHA * gate)
        y = _dot(act.astype(BF16), wd_bf[...]) + bd_ref[0]
        y_ref[...] = g_ref[...] * y

    @pl.when(jnp.logical_not(used))
    def _():
        y_ref[...] = jnp.zeros_like(y_ref)


def _experts(x_sorted, gate_sorted, tile_expert, n_used, wgu, bgu, wd, bd):
    tm = MOE_TILE
    n_tiles = x_sorted.shape[0] // tm
    dff2 = wgu.shape[2]
    grid_spec = pltpu.PrefetchScalarGridSpec(
        num_scalar_prefetch=2,
        grid=(n_tiles,),
        in_specs=[pl.BlockSpec((tm, D_MODEL), lambda i, te, n: (i, 0)),
                  pl.BlockSpec((1, D_MODEL, dff2), lambda i, te, n: (te[i], 0, 0)),
                  pl.BlockSpec((1, 1, dff2), lambda i, te, n: (te[i], 0, 0)),
                  pl.BlockSpec((1, dff2 // 2, D_MODEL), lambda i, te, n: (te[i], 0, 0)),
                  pl.BlockSpec((1, 1, D_MODEL), lambda i, te, n: (te[i], 0, 0)),
                  pl.BlockSpec((tm, 1), lambda i, te, n: (i, 0))],
        out_specs=pl.BlockSpec((tm, D_MODEL), lambda i, te, n: (i, 0)),
        scratch_shapes=[pltpu.VMEM((D_MODEL, dff2), BF16), pltpu.VMEM((dff2 // 2, D_MODEL), BF16)],
    )
    return pl.pallas_call(
        _expert_kernel,
        grid_spec=grid_spec,
        out_shape=jax.ShapeDtypeStruct((n_tiles * tm, D_MODEL), F32),
        compiler_params=_cparams(("arbitrary",)),
        name="moe_experts",
    )(tile_expert, n_used, x_sorted, wgu, bgu, wd, bd, gate_sorted)


def _combine_kernel(dest_ref, y_hbm, h_ref, lng_ref, lnb_ref, o_ref, buf, sem, *, tt, alpha):
    i = pl.program_id(0)

    def issue(r, carry):
        for k in range(TOP_K):
            _row_copy(y_hbm, buf.at[k], sem, dest_ref[(i * tt + r) * TOP_K + k], r).start()
        return carry

    lax.fori_loop(0, tt, issue, 0, unroll=4)
    for k in range(TOP_K):
        pltpu.make_async_copy(y_hbm.at[pl.ds(0, tt), :], buf.at[k], sem).wait()
    ffn = buf[0] + buf[1] + buf[2] + buf[3]
    o_ref[...] = _layer_norm(alpha * h_ref[...] + ffn, lng_ref[...], lnb_ref[...])


def _combine(y_sorted, dest, h, lng, lnb, alpha):
    t = h.shape[0]
    tt = _pick_tile(t, (128, 64, 16, 8))
    grid_spec = pltpu.PrefetchScalarGridSpec(
        num_scalar_prefetch=1,
        grid=(t // tt,),
        in_specs=[pl.BlockSpec(memory_space=pl.ANY),
                  pl.BlockSpec((tt, D_MODEL), lambda i, d: (i, 0)),
                  pl.BlockSpec((1, D_MODEL), lambda i, d: (0, 0)),
                  pl.BlockSpec((1, D_MODEL), lambda i, d: (0, 0))],
        out_specs=pl.BlockSpec((tt, D_MODEL), lambda i, d: (i, 0)),
        scratch_shapes=[pltpu.VMEM((TOP_K, tt, D_MODEL), F32), pltpu.SemaphoreType.DMA],
    )
    return pl.pallas_call(
        functools.partial(_combine_kernel, tt=tt, alpha=alpha),
        grid_spec=grid_spec,
        out_shape=jax.ShapeDtypeStruct((t, D_MODEL), F32),
        compiler_params=_cparams(("arbitrary",)),
        name="moe_combine",
    )(dest, y_sorted, h, lng, lnb)


def _moe_plan(eidx, rank, gates, counts, n_tiles, expert0):
    tm = MOE_TILE
    cnt = counts[0, :N_EXPERTS]
    padded = (cnt + tm - 1) // tm * tm
    ends = jnp.cumsum(padded)
    starts = ends - padded
    dest = (starts[eidx] + rank).reshape(-1).astype(jnp.int32)
    t = eidx.shape[0]
    tok = jnp.repeat(jnp.arange(t, dtype=jnp.int32), TOP_K)
    src_tok = jnp.zeros((n_tiles * tm,), jnp.int32).at[dest].set(tok)
    gate_sorted = jnp.zeros((n_tiles * tm, 1), F32).at[dest, 0].set(gates.reshape(-1))
    tile_start = jnp.arange(n_tiles, dtype=jnp.int32) * tm
    tile_expert = jnp.minimum(jnp.searchsorted(ends, tile_start, side="right"), N_EXPERTS - 1)
    tile_expert = (tile_expert + expert0).astype(jnp.int32)
    n_used = (ends[-1] // tm).astype(jnp.int32).reshape(1)
    return dest, src_tok, gate_sorted, tile_expert, n_used


def _layer(x2d, lw, shared, layer, depth, dims, alpha, kv_prev):
    nb_p, seq_p, nb_s, seq_s = dims
    t_p = nb_p * seq_p
    t_s = nb_s * seq_s
    page_table = shared["page_table"]
    page0 = layer * shared["n_pool"]
    proj = _project(x2d, lw["wt_main"])
    kbuf, vbuf = _kv_prompt(x2d, lw["wt_kv"], layer, depth, nb_p, seq_p, kv_prev)
    kv_s = _project(x2d, lw["wt_kv"], row0=t_p, rows=t_s)
    logft_p, cum_p, cumt_p = _fox_gate_prompt(x2d, lw["wt_f"], lw["b_f"], nb_p, seq_p)
    cumt_past, logft_s, cumt_s = _fox_gate_sample(x2d, lw["wt_f"], lw["b_f"], shared["cache_logf_t"], page_table,
                                                  page0, t_p, nb_s, seq_s)
    oa_p, s_p = _hgrn(proj, lw["lb"], lw["hgrn_g"], None, 0, 0, nb_p, seq_p)
    oa_s, s_s = _hgrn(proj, lw["lb"], lw["hgrn_g"], shared["state"], layer * nb_s, t_p, nb_s, seq_s)
    ob_p = _fox_prompt(proj, kbuf, vbuf, layer, cum_p, cumt_p, nb_p, seq_p)
    ob_s = _fox_sample(proj, kv_s, cumt_s, cumt_past, shared["cache_kt"], shared["cache_vt"], page_table,
                       page0, t_p, nb_s, seq_s)
    h, eidx, gates, rank, counts = _mix(oa_p, oa_s, ob_p, ob_s, proj, x2d, lw["w_a"], lw["w_b"], lw["w_o"],
                                        lw["ln_mix_g"], lw["ln_mix_b"], lw["wr_hi"], lw["wr_lo"], lw["b_r"], alpha)
    t = x2d.shape[0]
    n_tiles = (t * TOP_K + N_EXPERTS * (MOE_TILE - 1)) // MOE_TILE + 1
    dest, src_tok, gate_sorted, tile_expert, n_used = _moe_plan(eidx, rank, gates, counts, n_tiles,
                                                                layer * N_EXPERTS)
    x_sorted = _dispatch(h, src_tok, n_used, n_tiles)
    y_sorted = _experts(x_sorted, gate_sorted, tile_expert, n_used,
                        shared["w_gate_up"], shared["b_gate_up"], shared["w_down"], shared["b_down"])
    y = _combine(y_sorted, dest, h, lw["ln_ffn_g"], lw["ln_ffn_b"], alpha)
    return y, (kbuf, vbuf), kv_s, logft_p, logft_s, s_p, s_s


def kernel(x_prompt, x_sample, cache_k, cache_v, cache_logf, state_hgrn, page_table, w_in, b_fox_f, lb_logits,
           hgrn_norm_g, w_branch_a, w_branch_b, w_out, ln_mix_g, ln_mix_b, w_router, b_router, w_gate_up,
           b_gate_up, w_down, b_down, ln_ffn_g, ln_ffn_b):
    depth = w_in.shape[0]
    nb_p, seq_p, d = x_prompt.shape
    nb_s, seq_s, _ = x_sample.shape
    t_p, t_s = nb_p * seq_p, nb_s * seq_s
    alpha = (2 * depth) ** 0.25
    n_pool, page = cache_k.shape[1], cache_k.shape[2]
    dff2 = w_gate_up.shape[-1]

    pl_soft = jax.nn.softmax(lb_logits.astype(F32), axis=0)
    lower_bounds = jnp.cumsum(pl_soft, axis=0) - pl_soft[0:1]

    x2d = jnp.concatenate([x_prompt.reshape(t_p, d), x_sample.reshape(t_s, d)], axis=0)
    shared = {
        "page_table": page_table, "n_pool": n_pool,
        "cache_kt": cache_k.transpose(0, 1, 3, 4, 2).reshape(depth * n_pool, D_MODEL, page),
        "cache_vt": cache_v.transpose(0, 1, 3, 4, 2).reshape(depth * n_pool, D_MODEL, page),
        "cache_logf_t": cache_logf.transpose(0, 1, 3, 2).reshape(depth * n_pool, B_HEADS, page),
        "state": state_hgrn.reshape(depth * nb_s, A_HEADS, A_DK, A_DK),
        "w_gate_up": w_gate_up.reshape(depth * N_EXPERTS, d, dff2),
        "b_gate_up": b_gate_up.reshape(depth * N_EXPERTS, 1, dff2),
        "w_down": w_down.reshape(depth * N_EXPERTS, dff2 // 2, d),
        "b_down": b_down.reshape(depth * N_EXPERTS, 1, d),
    }
    wt_in = w_in.transpose(0, 2, 1)
    c_k = 5 * D_MODEL
    c_f = 7 * D_MODEL
    pad_r = LANES - N_EXPERTS
    kv_prev = None
    fp, sp, ks, vs, fs, ss = [], [], [], [], [], []
    for l in range(depth):
        wt = wt_in[l]
        wr_t = jnp.pad(w_router[l].T, ((0, pad_r), (0, 0)))
        wr_hi = wr_t.astype(BF16)
        lw = {
            "wt_main": jnp.concatenate([wt[:c_k], wt[c_f + B_HEADS:]], axis=0).astype(BF16),
            "wt_kv": wt[c_k:c_f].astype(BF16),
            "wt_f": wt[c_f:c_f + B_HEADS].astype(BF16),
            "b_f": b_fox_f[l],
            "lb": lower_bounds[l].reshape(1, d),
            "hgrn_g": hgrn_norm_g[l].reshape(1, A_DK),
            "w_a": w_branch_a[l].astype(BF16), "w_b": w_branch_b[l].astype(BF16), "w_o": w_out[l].astype(BF16),
            "ln_mix_g": ln_mix_g[l].reshape(1, d), "ln_mix_b": ln_mix_b[l].reshape(1, d),
            "ln_ffn_g": ln_ffn_g[l].reshape(1, d), "ln_ffn_b": ln_ffn_b[l].reshape(1, d),
            "b_r": jnp.pad(b_router[l], (0, pad_r), constant_values=MASK_VALUE).reshape(1, LANES),
            "wr_hi": wr_hi, "wr_lo": (wr_t - wr_hi.astype(F32)).astype(BF16),
        }
        x2d, kv_prev, kv_s, logft_p, logft_s, s_p, s_s = _layer(
            x2d, lw, shared, l, depth, (nb_p, seq_p, nb_s, seq_s), alpha, kv_prev)
        ks.append(kv_s[:, :D_MODEL].reshape(nb_s, seq_s, B_HEADS, B_DH))
        vs.append(kv_s[:, D_MODEL:].reshape(nb_s, seq_s, B_HEADS, B_DH))
        fp.append(logft_p)
        fs.append(logft_s)
        sp.append(s_p)
        ss.append(s_s)
    y_p = x2d[:t_p].reshape(nb_p, seq_p, d)
    y_s = x2d[t_p:].reshape(nb_s, seq_s, d)
    kbuf, vbuf = kv_prev
    k_p = kbuf.reshape(depth, nb_p, B_HEADS, B_DH, seq_p).transpose(0, 1, 4, 2, 3)
    v_p = vbuf.reshape(depth, nb_p, B_HEADS, B_DH, seq_p).transpose(0, 1, 4, 2, 3)
    f_p = jnp.stack(fp).transpose(0, 1, 3, 2)
    f_s = jnp.stack(fs).transpose(0, 1, 3, 2)
    return (y_p, y_s, k_p, v_p, f_p, jnp.stack(sp), jnp.stack(ks), jnp.stack(vs), f_s, jnp.stack(ss))
```

```python
import functools

import jax
import jax.numpy as jnp
from jax import lax
from jax.experimental import pallas as pl
from jax.experimental.pallas import tpu as pltpu

F32 = jnp.float32
BF16 = jnp.bfloat16

D_MODEL = 1024
A_HEADS = 8
A_DK = 128
B_HEADS = 16
B_DH = 64
N_EXPERTS = 32
TOP_K = 4
SWIGLU_LIMIT = 7.0
SWIGLU_ALPHA = 1.702
LN_EPS = 1e-5
RMS_EPS = 1e-6
MASK_VALUE = -1e30
LANES = 128
HGRN_CHUNK = 32
HGRN_SAFE_DECAY = 60.0
VMEM_LIMIT = 56 * 1024 * 1024
PROJ_Q0 = 4 * D_MODEL
PROJ_G0 = 5 * D_MODEL
MOE_TILE = 256


def _pick_tile(n, candidates):
    for c in candidates:
        if n % c == 0:
            return c
    return n


def _cparams(sem, vmem=None):
    return pltpu.CompilerParams(dimension_semantics=sem, vmem_limit_bytes=vmem or VMEM_LIMIT)


def _split3(x):
    hi = x.astype(BF16)
    r = x - hi.astype(F32)
    mid = r.astype(BF16)
    lo = (r - mid.astype(F32)).astype(BF16)
    return hi, mid, lo


def _dot(a, b):
    return jnp.dot(a, b, preferred_element_type=F32)


def _dot_nt(a, b):
    return lax.dot_general(a, b, (((1,), (1,)), ((), ())), preferred_element_type=F32)


def _dot_tn(a, b):
    return lax.dot_general(a, b, (((0,), (0,)), ((), ())), preferred_element_type=F32)


def _tri(n, upper=False):
    r = lax.broadcasted_iota(jnp.int32, (n, n), 0)
    c = lax.broadcasted_iota(jnp.int32, (n, n), 1)
    return ((r <= c) if upper else (c <= r)).astype(BF16)


def _cumsum_rows(x):
    tri = _tri(x.shape[0])
    hi, mid, lo = _split3(x)
    return _dot(tri, hi) + _dot(tri, mid) + _dot(tri, lo)


def _cumsum_lanes(x):
    tri = _tri(x.shape[1], upper=True)
    hi, mid, lo = _split3(x)
    return _dot(hi, tri) + _dot(mid, tri) + _dot(lo, tri)


def _log_sigmoid(z):
    return jnp.minimum(z, 0.0) - jnp.log(1.0 + jnp.exp(-jnp.abs(z)))


def _sigmoid(z):
    return 1.0 / (1.0 + jnp.exp(-z))


def _silu(z):
    return z * _sigmoid(z)


def _layer_norm(x, g, b):
    mu = jnp.mean(x, axis=-1, keepdims=True)
    xc = x - mu
    var = jnp.mean(xc * xc, axis=-1, keepdims=True)
    return xc * lax.rsqrt(var + LN_EPS) * g + b


def _drop_ref(kern, idx):
    def wrapped(*refs):
        return kern(*(refs[:idx] + refs[idx + 1:]))
    return wrapped


def _proj_kernel(x_ref, wt_ref, o_ref):
    o_ref[...] = _dot_nt(x_ref[...].astype(BF16), wt_ref[...])


def _project(x2d, wt, row0=0, rows=None):
    d = x2d.shape[1]
    t = x2d.shape[0] if rows is None else rows
    n = wt.shape[0]
    tm = _pick_tile(t, (1280, 640, 512, 256))
    tn = _pick_tile(n, (1024, 512, 256, 128))
    rb0 = row0 // tm
    return pl.pallas_call(
        _proj_kernel,
        grid=(n // tn, t // tm),
        in_specs=[pl.BlockSpec((tm, d), lambda j, i: (rb0 + i, 0)),
                  pl.BlockSpec((tn, d), lambda j, i: (j, 0))],
        out_specs=pl.BlockSpec((tm, tn), lambda j, i: (i, j)),
        out_shape=jax.ShapeDtypeStruct((t, n), F32),
        compiler_params=_cparams(("parallel", "parallel")),
        name="in_proj",
    )(x2d, wt)


def _kv_prompt_kernel(x_ref, wt_ref, k_ref, v_ref, *, layer, fill_others):
    def compute():
        kv = _dot_nt(wt_ref[...], x_ref[...].astype(BF16))
        k_ref[0, 0] = kv[:D_MODEL]
        v_ref[0, 0] = kv[D_MODEL:]

    if not fill_others:
        compute()
        return
    s = pl.program_id(0)
    pl.when(s == layer)(compute)

    @pl.when(s != layer)
    def _():
        k_ref[...] = jnp.zeros_like(k_ref)
        v_ref[...] = jnp.zeros_like(v_ref)


def _kv_prompt(x2d, wt_kv, layer, depth, nb, seq, prev):
    d = x2d.shape[1]
    tl = _pick_tile(seq, (512, 256, 128))
    nl = seq // tl
    first = prev is None
    in_specs = [pl.BlockSpec((tl, d), lambda s, b, i: (b * nl + i, 0)),
                pl.BlockSpec((2 * D_MODEL, d), lambda s, b, i: (0, 0))]
    args = [x2d, wt_kv]
    kern = functools.partial(_kv_prompt_kernel, layer=layer, fill_others=first)
    aliases = {}
    if first:
        ospec = pl.BlockSpec((1, 1, D_MODEL, tl), lambda s, b, i: (s, b, 0, i))
    else:
        in_specs += [pl.BlockSpec(memory_space=pl.ANY)] * 2
        args += list(prev)
        aliases = {2: 0, 3: 1}
        kern = _drop_ref(_drop_ref(kern, 2), 3)
        ospec = pl.BlockSpec((1, 1, D_MODEL, tl), lambda s, b, i: (layer, b, 0, i))
    out = jax.ShapeDtypeStruct((depth, nb, D_MODEL, seq), F32)
    return pl.pallas_call(
        kern,
        grid=(depth if first else 1, nb, nl),
        in_specs=in_specs,
        out_specs=[ospec, ospec],
        out_shape=[out, out],
        input_output_aliases=aliases,
        compiler_params=_cparams(("parallel", "parallel", "parallel")),
        name="kv_prompt",
    )(*args)


def _fox_gate_prompt_kernel(x_ref, wft_ref, bf_ref, bft_ref, logft_ref, cum_ref, cumt_ref, carry_ref, carryt_ref):
    i = pl.program_id(1)

    @pl.when(i == 0)
    def _():
        carry_ref[...] = jnp.zeros_like(carry_ref)
        carryt_ref[...] = jnp.zeros_like(carryt_ref)

    xb = x_ref[...].astype(BF16)
    logf = _log_sigmoid(_dot_nt(xb, wft_ref[...]) + bf_ref[...])
    logft = _log_sigmoid(_dot_nt(wft_ref[...], xb) + bft_ref[...])
    tl = logf.shape[0]
    cs = _cumsum_rows(logf) + carry_ref[...]
    cst = _cumsum_lanes(logft) + carryt_ref[...]
    carry_ref[...] = cs[tl - 1:tl, :]
    carryt_ref[...] = cst[:, tl - 1:tl]
    logft_ref[0] = logft
    cum_ref[0] = cs
    cumt_ref[0] = cst


def _fox_gate_prompt(x2d, wft, bf, nb, seq):
    d = x2d.shape[1]
    tl = _pick_tile(seq, (256, 128))
    nl = seq // tl
    out_t = jax.ShapeDtypeStruct((nb, B_HEADS, seq), F32)
    tspec = pl.BlockSpec((1, B_HEADS, tl), lambda b, i: (b, 0, i))
    return pl.pallas_call(
        _fox_gate_prompt_kernel,
        grid=(nb, nl),
        in_specs=[pl.BlockSpec((tl, d), lambda b, i: (b * nl + i, 0)),
                  pl.BlockSpec((B_HEADS, d), lambda b, i: (0, 0)),
                  pl.BlockSpec((1, B_HEADS), lambda b, i: (0, 0)),
                  pl.BlockSpec((B_HEADS, 1), lambda b, i: (0, 0))],
        out_specs=[tspec, pl.BlockSpec((1, tl, B_HEADS), lambda b, i: (b, i, 0)), tspec],
        out_shape=[out_t, jax.ShapeDtypeStruct((nb, seq, B_HEADS), F32), out_t],
        scratch_shapes=[pltpu.VMEM((1, B_HEADS), F32), pltpu.VMEM((B_HEADS, 1), F32)],
        compiler_params=_cparams(("parallel", "arbitrary")),
        name="fox_gate_prompt",
    )(x2d, wft, bf.reshape(1, B_HEADS), bf.reshape(B_HEADS, 1))


def _hgrn_kernel(*refs, chunk, n_chunks, has_s0):
    if has_s0:
        (aq_ref, af_ref, ai_ref, ag_ref, lb_ref, g_ref, s0_ref, o_ref, snew_ref,
         st_ref, a_ref, gs_ref, ks_ref, qs_ref) = refs
    else:
        (aq_ref, af_ref, ai_ref, ag_ref, lb_ref, g_ref, o_ref, snew_ref,
         st_ref, a_ref, gs_ref, ks_ref, qs_ref) = refs
        s0_ref = None
    i = pl.program_id(1)
    c = chunk

    @pl.when(i == 0)
    def _():
        for h in range(A_HEADS):
            if has_s0:
                st_ref[h] = s0_ref[0, h].T
            else:
                st_ref[h] = jnp.zeros((A_DK, A_DK), F32)

    lbv = lb_ref[...]
    gv = g_ref[...]
    row = lax.broadcasted_iota(jnp.int32, (c, c), 0)
    col = lax.broadcasted_iota(jnp.int32, (c, c), 1)
    causal = col <= row

    def do_chunk(ci, carry):
        r = pl.ds(pl.multiple_of(ci * c, c), c)
        z = af_ref[r, :]
        f = lbv + (1.0 - lbv) * _sigmoid(z)
        logf = jnp.log(f)
        kk = (1.0 - lbv) * _sigmoid(-z)
        g_cum = _cumsum_rows(logf)
        gend = g_cum[c - 1:c, :]
        q = _silu(aq_ref[r, :]) * (A_DK ** -0.5)
        vb = ai_ref[r, :].astype(BF16)
        qt = (q * jnp.exp(g_cum)).astype(BF16)
        khat = (kk * jnp.exp(gend - g_cum)).astype(BF16)
        safe = jnp.min(gend) >= -HGRN_SAFE_DECAY

        kt = (kk * jnp.exp(-g_cum)).astype(BF16)
        for h in range(A_HEADS):
            hs = slice(h * A_DK, (h + 1) * A_DK)
            a_ref[h] = _dot_nt(qt[:, hs], kt[:, hs])

        @pl.when(jnp.logical_not(safe))
        def _():
            gs_ref[...] = g_cum
            ks_ref[...] = kk
            qs_ref[...] = q
            a_ref[...] = jnp.zeros_like(a_ref)

            def col_j(j, carry2):
                gj = gs_ref[pl.ds(j, 1), :]
                kj = ks_ref[pl.ds(j, 1), :]
                w = qs_ref[...] * kj * jnp.exp(jnp.minimum(gs_ref[...] - gj, 0.0))
                for h in range(A_HEADS):
                    hs = slice(h * A_DK, (h + 1) * A_DK)
                    cj = jnp.sum(w[:, hs], axis=1, keepdims=True)
                    a_ref[h] = a_ref[h] + jnp.where(col == j, cj, 0.0)
                return carry2

            lax.fori_loop(0, c, col_j, 0)

        egend = jnp.exp(gend)
        for h in range(A_HEADS):
            hs = slice(h * A_DK, (h + 1) * A_DK)
            att = jnp.where(causal, a_ref[h], 0.0).astype(BF16)
            st = st_ref[h]
            o = _dot(att, vb[:, hs]) + _dot_nt(qt[:, hs], st.astype(BF16))
            st_ref[h] = st * egend[:, hs] + _dot_tn(vb[:, hs], khat[:, hs])
            ms = jnp.mean(o * o, axis=-1, keepdims=True)
            o = o * lax.rsqrt(ms + RMS_EPS) * gv
            o_ref[r, hs] = o * _silu(ag_ref[r, hs])
        return carry

    lax.fori_loop(0, n_chunks, do_chunk, 0, unroll=min(2, n_chunks))

    @pl.when(i == pl.num_programs(1) - 1)
    def _():
        for h in range(A_HEADS):
            snew_ref[0, h] = st_ref[h].T


def _hgrn(proj, lb, g, s0, s0_base, row0, nb, seq):
    chunk = min(HGRN_CHUNK, seq)
    lb_rows = _pick_tile(seq, (256, chunk))
    nl = seq // lb_rows
    rb0 = row0 // lb_rows
    has_s0 = s0 is not None
    in_specs = [pl.BlockSpec((lb_rows, D_MODEL), functools.partial(lambda b, i, k: (rb0 + b * nl + i, k), k=k))
                for k in range(4)]
    in_specs += [pl.BlockSpec((1, D_MODEL), lambda b, i: (0, 0)),
                 pl.BlockSpec((1, A_DK), lambda b, i: (0, 0))]
    args = [proj, proj, proj, proj, lb, g]
    if has_s0:
        in_specs.append(pl.BlockSpec((1, A_HEADS, A_DK, A_DK), lambda b, i: (s0_base + b, 0, 0, 0)))
        args.append(s0)
    kern = functools.partial(_hgrn_kernel, chunk=chunk, n_chunks=lb_rows // chunk, has_s0=has_s0)
    return pl.pallas_call(
        kern,
        grid=(nb, nl),
        in_specs=in_specs,
        out_specs=[pl.BlockSpec((lb_rows, D_MODEL), lambda b, i: (b * nl + i, 0)),
                   pl.BlockSpec((1, A_HEADS, A_DK, A_DK), lambda b, i: (b, 0, 0, 0))],
        out_shape=[jax.ShapeDtypeStruct((nb * seq, D_MODEL), F32),
                   jax.ShapeDtypeStruct((nb, A_HEADS, A_DK, A_DK), F32)],
        scratch_shapes=[pltpu.VMEM((A_HEADS, A_DK, A_DK), F32),
                        pltpu.VMEM((A_HEADS, chunk, chunk), F32),
                        pltpu.VMEM((chunk, D_MODEL), F32),
                        pltpu.VMEM((chunk, D_MODEL), F32),
                        pltpu.VMEM((chunk, D_MODEL), F32)],
        compiler_params=_cparams(("parallel", "arbitrary")),
        name="hgrn",
    )(*args)


def _fox_prompt_kernel(q_ref, k_ref, v_ref, cum_ref, cumt_ref, o_ref, kn_ref, qt_ref, vt_ref, *, tq, tk):
    j = pl.program_id(1)
    seq = q_ref.shape[0]
    nq = seq // tq
    hpair = (2 * j, 2 * j + 1)

    kn_ref[:, :LANES] = k_ref[0, 0].T.astype(BF16)
    h16 = lax.broadcasted_iota(jnp.int32, (B_HEADS, LANES), 0)
    l16 = lax.broadcasted_iota(jnp.int32, (B_HEADS, LANES), 1)
    kaug = jnp.zeros((seq, LANES), F32)
    for part, term in enumerate(_split3(cum_ref[0])):
        sel = jnp.where(((h16 == hpair[0]) & (l16 == part)) | ((h16 == hpair[1]) & (l16 == 3 + part)), -1.0, 0.0)
        kaug = kaug + _dot(term, sel.astype(BF16))
    lane_s = lax.broadcasted_iota(jnp.int32, (seq, LANES), 1)
    kn_ref[:, LANES:] = (kaug + jnp.where((lane_s >= 6) & (lane_s < 9), 1.0, 0.0)).astype(BF16)

    qt = (q_ref[...] * (B_DH ** -0.5)).T
    row = lax.broadcasted_iota(jnp.int32, (LANES, seq), 0)
    r16 = lax.broadcasted_iota(jnp.int32, (LANES, B_HEADS), 0)
    c16 = lax.broadcasted_iota(jnp.int32, (LANES, B_HEADS), 1)
    cumt_terms = _split3(cumt_ref[0])
    for h in range(2):
        qt_ref[h, :LANES, :] = jnp.where(row // B_DH == h, qt, 0.0).astype(BF16)
        qaug = jnp.where((row >= 3 * h) & (row < 3 * h + 3), 1.0, 0.0)
        for part, term in enumerate(cumt_terms):
            sel = jnp.where((r16 == 6 + part) & (c16 == hpair[h]), 1.0, 0.0)
            qaug = qaug + _dot(sel.astype(BF16), term)
        qt_ref[h, LANES:, :] = qaug.astype(BF16)
    vt_ref[...] = v_ref[0, 0].astype(BF16)

    kpos = lax.broadcasted_iota(jnp.int32, (tk, tq), 0)
    qpos = lax.broadcasted_iota(jnp.int32, (tk, tq), 1)
    kpq = tq // tk

    def q_block(qi, carry):
        qs = pl.ds(pl.multiple_of(qi * tq, tq), tq)
        qa = [qt_ref[h, :, qs] for h in range(2)]

        def kv_step(ki, state, masked):
            ks = pl.ds(pl.multiple_of(ki * tk, tk), tk)
            kb = kn_ref[ks, :]
            new_state = []
            for h in range(2):
                m, l, acc = state[h]
                s = _dot(kb, qa[h])
                if masked:
                    s = jnp.where(ki * tk + kpos <= qi * tq + qpos, s, MASK_VALUE)
                m_new = jnp.maximum(m, jnp.max(s, axis=0, keepdims=True))
                alpha = jnp.exp(m - m_new)
                p = jnp.exp(s - m_new)
                l = alpha * l + jnp.sum(p, axis=0, keepdims=True)
                acc = alpha * acc + _dot(vt_ref[h * B_DH:(h + 1) * B_DH, ks], p.astype(BF16))
                new_state.append((m_new, l, acc))
            return tuple(new_state)

        init = tuple((jnp.full((1, tq), -jnp.inf, F32), jnp.zeros((1, tq), F32), jnp.zeros((B_DH, tq), F32))
                     for _ in range(2))
        state = lax.fori_loop(0, qi * kpq, lambda ki, st: kv_step(ki, st, False), init)
        for dk in range(kpq):
            state = kv_step(qi * kpq + dk, state, True)
        ot = jnp.concatenate([acc * (1.0 / l) for (_, l, acc) in state], axis=0)
        o_ref[qs, :] = ot.T
        return carry

    lax.fori_loop(0, nq, q_block, 0)


def _fox_prompt(proj, kbuf, vbuf, layer, cum, cum_t, nb, seq):
    t = nb * seq
    tq = _pick_tile(seq, (512, 256, 128))
    tk = min(tq, 256)
    npair = B_HEADS // 2
    c0 = PROJ_Q0 // LANES
    kvspec = pl.BlockSpec((1, 1, LANES, seq), lambda b, j: (layer, b, j, 0))
    return pl.pallas_call(
        functools.partial(_fox_prompt_kernel, tq=tq, tk=tk),
        grid=(nb, npair),
        in_specs=[pl.BlockSpec((seq, LANES), lambda b, j: (b, c0 + j)), kvspec, kvspec,
                  pl.BlockSpec((1, seq, B_HEADS), lambda b, j: (b, 0, 0)),
                  pl.BlockSpec((1, B_HEADS, seq), lambda b, j: (b, 0, 0))],
        out_specs=pl.BlockSpec((seq, LANES), lambda b, j: (b, j)),
        out_shape=jax.ShapeDtypeStruct((t, D_MODEL), F32),
        scratch_shapes=[pltpu.VMEM((seq, 2 * LANES), BF16), pltpu.VMEM((2, 2 * LANES, seq), BF16),
                        pltpu.VMEM((LANES, seq), BF16)],
        compiler_params=_cparams(("parallel", "parallel")),
        name="fox_prompt",
    )(proj, kbuf, vbuf, cum, cum_t)


def _fox_sample_kernel(*refs, pages_per_step, ns):
    pps = pages_per_step
    q_ref, kvn_ref, x_ref, wft_ref, bft_ref = refs[1:6]
    k_refs = refs[6:6 + pps]
    v_refs = refs[6 + pps:6 + 2 * pps]
    f_refs = refs[6 + 2 * pps:6 + 3 * pps]
    o_ref, logft_ref = refs[6 + 3 * pps:8 + 3 * pps]
    qbd_ref, fq_ref, cn_ref, past_ref, m_ref, l_ref, acc_ref = refs[8 + 3 * pps:]
    g = pl.program_id(1)
    nrow = B_HEADS * ns
    rh = lax.broadcasted_iota(jnp.int32, (nrow, D_MODEL), 0) // ns
    ch = lax.broadcasted_iota(jnp.int32, (nrow, D_MODEL), 1) // B_DH

    def expand_rows(x):
        n = x.shape[1]
        return jnp.broadcast_to(x[:, None, :], (B_HEADS, ns, n)).reshape(nrow, n)

    @pl.when(g == 0)
    def _():
        q = q_ref[...] * (B_DH ** -0.5)
        qrep = jnp.broadcast_to(q[None], (B_HEADS, ns, D_MODEL)).reshape(nrow, D_MODEL)
        qbd_ref[...] = jnp.where(rh == ch, qrep, 0.0).astype(BF16)
        logft = _log_sigmoid(_dot_nt(wft_ref[...], x_ref[...].astype(BF16)) + bft_ref[...])
        logft_ref[0] = logft
        cn = _cumsum_lanes(logft)
        cn_ref[...] = cn
        tsel = (lax.broadcasted_iota(jnp.int32, (nrow, ns), 0) % ns
                == lax.broadcasted_iota(jnp.int32, (nrow, ns), 1))
        fq_ref[...] = jnp.sum(jnp.where(tsel, expand_rows(cn), 0.0), axis=1, keepdims=True)
        past_ref[...] = jnp.zeros_like(past_ref)
        m_ref[...] = jnp.full_like(m_ref, -jnp.inf)
        l_ref[...] = jnp.zeros_like(l_ref)
        acc_ref[...] = jnp.zeros_like(acc_ref)

    def update(s, pv):
        m_old = m_ref[...]
        m_new = jnp.maximum(m_old, jnp.max(s, axis=1, keepdims=True))
        alpha = jnp.exp(m_old - m_new)
        p = jnp.exp(s - m_new)
        l_ref[...] = alpha * l_ref[...] + jnp.sum(p, axis=1, keepdims=True)
        acc_ref[...] = alpha * acc_ref[...] + pv(p.astype(BF16))
        m_ref[...] = m_new

    page = k_refs[0].shape[2]
    lf = jnp.concatenate([f_refs[pi][0] for pi in range(pps)], axis=0)
    cs = _cumsum_lanes(lf)
    offset = past_ref[...]
    s_parts = []
    for pi in range(pps):
        cs_p = cs[pi * B_HEADS:(pi + 1) * B_HEADS, :] + offset
        offset = cs_p[:, page - 1:]
        kt = k_refs[pi][0].astype(BF16)
        s_parts.append(_dot(qbd_ref[...], kt) + fq_ref[...] - expand_rows(cs_p))
    past_ref[...] = offset

    def pv_pages(p):
        out = None
        for pi in range(pps):
            term = _dot_nt(p[:, pi * page:(pi + 1) * page], v_refs[pi][0].astype(BF16))
            out = term if out is None else out + term
        return out

    update(jnp.concatenate(s_parts, axis=1), pv_pages)

    @pl.when(g == pl.num_programs(1) - 1)
    def _():
        kb = kvn_ref[:, :D_MODEL].astype(BF16)
        vb = kvn_ref[:, D_MODEL:].astype(BF16)
        fk = expand_rows(cn_ref[...] + past_ref[...])
        s = _dot_nt(qbd_ref[...], kb) + fq_ref[...] - fk
        tq = lax.broadcasted_iota(jnp.int32, (nrow, ns), 0) % ns
        tk = lax.broadcasted_iota(jnp.int32, (nrow, ns), 1)
        s = jnp.where(tq >= tk, s, MASK_VALUE)
        update(s, lambda p: _dot(p, vb))
        o = acc_ref[...] / l_ref[...]
        for h in range(B_HEADS):
            o_ref[:, h * B_DH:(h + 1) * B_DH] = o[h * ns:(h + 1) * ns, h * B_DH:(h + 1) * B_DH]


def _fox_sample(proj, kv_new, x2d, wft, bf, cache_kt, cache_vt, cache_logf_t, page_table, page0, row0, nb, ns):
    d = x2d.shape[1]
    n_pages = page_table.shape[1]
    page = cache_kt.shape[2]
    pps = _pick_tile(n_pages, (8, 4, 2, 1))
    ng = n_pages // pps
    rb0 = row0 // ns
    nrow = B_HEADS * ns

    def page_spec(pi, rows):
        return pl.BlockSpec((1, rows, page), lambda b, g, pt: (page0 + pt[b, g * pps + pi], 0, 0))

    in_specs = [pl.BlockSpec((ns, D_MODEL), lambda b, g, pt: (rb0 + b, PROJ_Q0 // D_MODEL)),
                pl.BlockSpec((ns, 2 * D_MODEL), lambda b, g, pt: (b, 0)),
                pl.BlockSpec((ns, d), lambda b, g, pt: (rb0 + b, 0)),
                pl.BlockSpec((B_HEADS, d), lambda b, g, pt: (0, 0)),
                pl.BlockSpec((B_HEADS, 1), lambda b, g, pt: (0, 0))]
    in_specs += [page_spec(pi, D_MODEL) for pi in range(pps)] * 2
    in_specs += [page_spec(pi, B_HEADS) for pi in range(pps)]
    args = [proj, kv_new, x2d, wft, bf.reshape(B_HEADS, 1)]
    args += [cache_kt] * pps + [cache_vt] * pps + [cache_logf_t] * pps
    grid_spec = pltpu.PrefetchScalarGridSpec(
        num_scalar_prefetch=1,
        grid=(nb, ng),
        in_specs=in_specs,
        out_specs=[pl.BlockSpec((ns, D_MODEL), lambda b, g, pt: (b, 0)),
                   pl.BlockSpec((1, B_HEADS, ns), lambda b, g, pt: (b, 0, 0))],
        scratch_shapes=[pltpu.VMEM((nrow, D_MODEL), BF16), pltpu.VMEM((nrow, 1), F32),
                        pltpu.VMEM((B_HEADS, ns), F32), pltpu.VMEM((B_HEADS, 1), F32),
                        pltpu.VMEM((nrow, 1), F32), pltpu.VMEM((nrow, 1), F32),
                        pltpu.VMEM((nrow, D_MODEL), F32)],
    )
    return pl.pallas_call(
        functools.partial(_fox_sample_kernel, pages_per_step=pps, ns=ns),
        grid_spec=grid_spec,
        out_shape=[jax.ShapeDtypeStruct((nb * ns, D_MODEL), F32),
                   jax.ShapeDtypeStruct((nb, B_HEADS, ns), F32)],
        compiler_params=_cparams(("parallel", "arbitrary")),
        name="fox_sample",
    )(page_table, *args)


def _mix_kernel(oap_ref, oas_ref, obp_ref, obs_ref, ga_ref, gb_ref, x_ref, wa_ref, wb_ref, wo_ref, lng_ref,
                lnb_ref, wrh_ref, wrl_ref, br_ref, h_ref, eidx_ref, gate_ref, rank_ref, cnt_ref, carry_ref,
                *, alpha, n_prompt_tiles):
    i = pl.program_id(0)

    @pl.when(i == 0)
    def _():
        carry_ref[...] = jnp.zeros_like(carry_ref)

    is_prompt = i < n_prompt_tiles
    oa = jnp.where(is_prompt, oap_ref[...], oas_ref[...])
    ob = jnp.where(is_prompt, obp_ref[...], obs_ref[...])
    ya = _dot(oa.astype(BF16), wa_ref[...])
    yb = _dot(ob.astype(BF16), wb_ref[...])
    m = _sigmoid(ga_ref[...]) * ya + _sigmoid(gb_ref[...]) * yb
    mix = _dot(m.astype(BF16), wo_ref[...])
    h = _layer_norm(alpha * x_ref[...] + mix, lng_ref[...], lnb_ref[...])
    h_ref[...] = h

    h_hi = h.astype(BF16)
    h_lo = (h - h_hi.astype(F32)).astype(BF16)
    logits = (_dot_nt(h_hi, wrh_ref[...]) + _dot_nt(h_lo, wrh_ref[...]) + _dot_nt(h_hi, wrl_ref[...])
              + br_ref[...])
    tm = logits.shape[0]
    lane = lax.broadcasted_iota(jnp.int32, (tm, LANES), 1)
    work = logits
    idxs, vals = [], []
    for _ in range(TOP_K):
        mx = jnp.max(work, axis=1, keepdims=True)
        ix = jnp.min(jnp.where(work == mx, lane, LANES), axis=1, keepdims=True)
        idxs.append(ix)
        vals.append(mx)
        work = jnp.where(lane == ix, -jnp.inf, work)
    exps = [jnp.exp(v - vals[0]) for v in vals]
    denom = exps[0] + exps[1] + exps[2] + exps[3]
    multihot = jnp.zeros((tm, LANES), F32)
    for ix in idxs:
        multihot = multihot + jnp.where(lane == ix, 1.0, 0.0)
    r = lax.broadcasted_iota(jnp.int32, (tm, tm), 0)
    c = lax.broadcasted_iota(jnp.int32, (tm, tm), 1)
    before = _dot((c < r).astype(BF16), multihot.astype(BF16)) + carry_ref[...]
    carry_ref[...] = carry_ref[...] + jnp.sum(multihot, axis=0, keepdims=True)
    col4 = lax.broadcasted_iota(jnp.int32, (tm, TOP_K), 1)
    eidx = jnp.zeros((tm, TOP_K), jnp.int32)
    gates = jnp.zeros((tm, TOP_K), F32)
    ranks = jnp.zeros((tm, TOP_K), F32)
    for k in range(TOP_K):
        rk = jnp.sum(jnp.where(lane == idxs[k], before, 0.0), axis=1, keepdims=True)
        eidx = jnp.where(col4 == k, idxs[k], eidx)
        gates = jnp.where(col4 == k, exps[k] / denom, gates)
        ranks = jnp.where(col4 == k, rk, ranks)
    eidx_ref[...] = eidx
    gate_ref[...] = gates
    rank_ref[...] = ranks.astype(jnp.int32)
    cnt_ref[...] = carry_ref[...].astype(jnp.int32)


def _mix(oa_p, oa_s, ob_p, ob_s, proj, x2d, wa, wb, wo, lng, lnb, wrh, wrl, br, alpha):
    t = x2d.shape[0]
    t_p, t_s = oa_p.shape[0], oa_s.shape[0]
    tm = next(c for c in (256, 128, 64, 32, 16, 8) if t_p % c == 0 and t_s % c == 0)
    npt = t_p // tm
    row = lambda i: (i, 0)
    prow = lambda i: (jnp.minimum(i, npt - 1), 0)
    srow = lambda i: (jnp.maximum(i - npt, 0), 0)
    const = lambda i: (0, 0)
    wspec = pl.BlockSpec((D_MODEL, D_MODEL), const)
    vspec = pl.BlockSpec((1, D_MODEL), const)
    return pl.pallas_call(
        functools.partial(_mix_kernel, alpha=alpha, n_prompt_tiles=npt),
        grid=(t // tm,),
        in_specs=[pl.BlockSpec((tm, D_MODEL), prow), pl.BlockSpec((tm, D_MODEL), srow),
                  pl.BlockSpec((tm, D_MODEL), prow), pl.BlockSpec((tm, D_MODEL), srow),
                  pl.BlockSpec((tm, D_MODEL), lambda i: (i, PROJ_G0 // D_MODEL)),
                  pl.BlockSpec((tm, D_MODEL), lambda i: (i, PROJ_G0 // D_MODEL + 1)),
                  pl.BlockSpec((tm, D_MODEL), row), wspec, wspec, wspec, vspec, vspec,
                  pl.BlockSpec((LANES, D_MODEL), const), pl.BlockSpec((LANES, D_MODEL), const),
                  pl.BlockSpec((1, LANES), const)],
        out_specs=[pl.BlockSpec((tm, D_MODEL), row), pl.BlockSpec((tm, TOP_K), row),
                   pl.BlockSpec((tm, TOP_K), row), pl.BlockSpec((tm, TOP_K), row),
                   pl.BlockSpec((1, LANES), const)],
        out_shape=[jax.ShapeDtypeStruct((t, D_MODEL), F32), jax.ShapeDtypeStruct((t, TOP_K), jnp.int32),
                   jax.ShapeDtypeStruct((t, TOP_K), F32), jax.ShapeDtypeStruct((t, TOP_K), jnp.int32),
                   jax.ShapeDtypeStruct((1, LANES), jnp.int32)],
        scratch_shapes=[pltpu.VMEM((1, LANES), F32)],
        compiler_params=_cparams(("arbitrary",)),
        name="mix_router",
    )(oa_p, oa_s, ob_p, ob_s, proj, proj, x2d, wa, wb, wo, lng, lnb, wrh, wrl, br)


def _dispatch_kernel(dest_ref, h_ref, xz_hbm, x_hbm, sem, *, tt):
    del xz_hbm
    i = pl.program_id(0)

    def issue(r, carry):
        for k in range(TOP_K):
            row = dest_ref[(i * tt + r) * TOP_K + k]
            pltpu.make_async_copy(h_ref.at[pl.ds(r, 1), :], x_hbm.at[pl.ds(row, 1), :], sem).start()
        return carry

    lax.fori_loop(0, tt, issue, 0, unroll=4)
    for _ in range(TOP_K):
        pltpu.make_async_copy(h_ref, x_hbm.at[pl.ds(0, tt), :], sem).wait()


def _dispatch(h, dest, n_rows):
    t = h.shape[0]
    tt = _pick_tile(t, (256, 128, 64, 16, 8))
    grid_spec = pltpu.PrefetchScalarGridSpec(
        num_scalar_prefetch=1,
        grid=(t // tt,),
        in_specs=[pl.BlockSpec((tt, D_MODEL), lambda i, d: (i, 0)),
                  pl.BlockSpec(memory_space=pl.ANY)],
        out_specs=pl.BlockSpec(memory_space=pl.ANY),
        scratch_shapes=[pltpu.SemaphoreType.DMA],
    )
    return pl.pallas_call(
        functools.partial(_dispatch_kernel, tt=tt),
        grid_spec=grid_spec,
        out_shape=jax.ShapeDtypeStruct((n_rows, D_MODEL), F32),
        input_output_aliases={2: 0},
        compiler_params=_cparams(("arbitrary",)),
        name="moe_dispatch",
    )(dest, h, jnp.zeros((n_rows, D_MODEL), F32))


def _expert_kernel(te_ref, nused_ref, x_ref, wgu_ref, bgu_ref, wd_ref, bd_ref, y_ref, wgu_bf, wd_bf):
    i = pl.program_id(0)
    used = i < nused_ref[0]
    prev = te_ref[jnp.maximum(i - 1, 0)]
    fresh = jnp.logical_or(i == 0, te_ref[i] != prev)

    @pl.when(jnp.logical_and(used, fresh))
    def _():
        wgu_bf[...] = wgu_ref[0].astype(BF16)
        wd_bf[...] = wd_ref[0].astype(BF16)

    @pl.when(used)
    def _():
        x = x_ref[...].astype(BF16)
        hcat = _dot(x, wgu_bf[...]) + bgu_ref[0]
        dff = hcat.shape[1] // 2
        gate = jnp.minimum(hcat[:, :dff], SWIGLU_LIMIT)
        up = jnp.clip(hcat[:, dff:], -SWIGLU_LIMIT, SWIGLU_LIMIT)
        act = (up + 1.0) * gate * _sigmoid(SWIGLU_ALPHA * gate)
        y_ref[...] = _dot(act.astype(BF16), wd_bf[...]) + bd_ref[0]

    @pl.when(jnp.logical_not(used))
    def _():
        y_ref[...] = jnp.zeros_like(y_ref)


def _experts(x_sorted, tile_expert, n_used, wgu, bgu, wd, bd):
    tm = MOE_TILE
    n_tiles = x_sorted.shape[0] // tm
    dff2 = wgu.shape[2]
    grid_spec = pltpu.PrefetchScalarGridSpec(
        num_scalar_prefetch=2,
        grid=(n_tiles,),
        in_specs=[pl.BlockSpec((tm, D_MODEL), lambda i, te, n: (i, 0)),
                  pl.BlockSpec((1, D_MODEL, dff2), lambda i, te, n: (te[i], 0, 0)),
                  pl.BlockSpec((1, 1, dff2), lambda i, te, n: (te[i], 0, 0)),
                  pl.BlockSpec((1, dff2 // 2, D_MODEL), lambda i, te, n: (te[i], 0, 0)),
                  pl.BlockSpec((1, 1, D_MODEL), lambda i, te, n: (te[i], 0, 0))],
        out_specs=pl.BlockSpec((tm, D_MODEL), lambda i, te, n: (i, 0)),
        scratch_shapes=[pltpu.VMEM((D_MODEL, dff2), BF16), pltpu.VMEM((dff2 // 2, D_MODEL), BF16)],
    )
    return pl.pallas_call(
        _expert_kernel,
        grid_spec=grid_spec,
        out_shape=jax.ShapeDtypeStruct((n_tiles * tm, D_MODEL), F32),
        compiler_params=_cparams(("arbitrary",)),
        name="moe_experts",
    )(tile_expert, n_used, x_sorted, wgu, bgu, wd, bd)


def _combine_kernel(dest_ref, y_hbm, gate_ref, h_ref, lng_ref, lnb_ref, o_ref, buf, sem, *, tt, alpha):
    i = pl.program_id(0)

    def issue(r, carry):
        for k in range(TOP_K):
            row = dest_ref[(i * tt + r) * TOP_K + k]
            pltpu.make_async_copy(y_hbm.at[pl.ds(row, 1), :], buf.at[k, pl.ds(r, 1), :], sem).start()
        return carry

    lax.fori_loop(0, tt, issue, 0, unroll=4)
    for k in range(TOP_K):
        pltpu.make_async_copy(y_hbm.at[pl.ds(0, tt), :], buf.at[k], sem).wait()
    gates = gate_ref[...]
    ffn = gates[:, 0:1] * buf[0]
    for k in range(1, TOP_K):
        ffn = ffn + gates[:, k:k + 1] * buf[k]
    o_ref[...] = _layer_norm(alpha * h_ref[...] + ffn, lng_ref[...], lnb_ref[...])


def _combine(y_sorted, dest, gates, h, lng, lnb, alpha):
    t = h.shape[0]
    tt = _pick_tile(t, (128, 64, 16, 8))
    grid_spec = pltpu.PrefetchScalarGridSpec(
        num_scalar_prefetch=1,
        grid=(t // tt,),
        in_specs=[pl.BlockSpec(memory_space=pl.ANY),
                  pl.BlockSpec((tt, TOP_K), lambda i, d: (i, 0)),
                  pl.BlockSpec((tt, D_MODEL), lambda i, d: (i, 0)),
                  pl.BlockSpec((1, D_MODEL), lambda i, d: (0, 0)),
                  pl.BlockSpec((1, D_MODEL), lambda i, d: (0, 0))],
        out_specs=pl.BlockSpec((tt, D_MODEL), lambda i, d: (i, 0)),
        scratch_shapes=[pltpu.VMEM((TOP_K, tt, D_MODEL), F32), pltpu.SemaphoreType.DMA],
    )
    return pl.pallas_call(
        functools.partial(_combine_kernel, tt=tt, alpha=alpha),
        grid_spec=grid_spec,
        out_shape=jax.ShapeDtypeStruct((t, D_MODEL), F32),
        compiler_params=_cparams(("arbitrary",)),
        name="moe_combine",
    )(dest, y_sorted, gates, h, lng, lnb)


def _moe_plan(eidx, rank, counts, n_tiles, expert0):
    tm = MOE_TILE
    cnt = counts[0, :N_EXPERTS]
    padded = (cnt + tm - 1) // tm * tm
    ends = jnp.cumsum(padded)
    starts = ends - padded
    dest = (starts[eidx] + rank).reshape(-1).astype(jnp.int32)
    tile_start = jnp.arange(n_tiles, dtype=jnp.int32) * tm
    tile_expert = jnp.sum((tile_start[:, None] >= ends[None, :]).astype(jnp.int32), axis=1)
    tile_expert = (jnp.minimum(tile_expert, N_EXPERTS - 1) + expert0).astype(jnp.int32)
    n_used = (ends[-1] // tm).astype(jnp.int32).reshape(1)
    return dest, tile_expert, n_used


def _layer(x2d, lw, shared, layer, depth, dims, alpha, kv_prev):
    nb_p, seq_p, nb_s, seq_s = dims
    t_p = nb_p * seq_p
    t_s = nb_s * seq_s
    page_table = shared["page_table"]
    page0 = layer * shared["n_pool"]
    proj = _project(x2d, lw["wt_main"])
    kbuf, vbuf = _kv_prompt(x2d, lw["wt_kv"], layer, depth, nb_p, seq_p, kv_prev)
    kv_s = _project(x2d, lw["wt_kv"], row0=t_p, rows=t_s)
    logft_p, cum_p, cumt_p = _fox_gate_prompt(x2d, lw["wt_f"], lw["b_f"], nb_p, seq_p)
    oa_p, s_p = _hgrn(proj, lw["lb"], lw["hgrn_g"], None, 0, 0, nb_p, seq_p)
    oa_s, s_s = _hgrn(proj, lw["lb"], lw["hgrn_g"], shared["state"], layer * nb_s, t_p, nb_s, seq_s)
    ob_p = _fox_prompt(proj, kbuf, vbuf, layer, cum_p, cumt_p, nb_p, seq_p)
    ob_s, logft_s = _fox_sample(proj, kv_s, x2d, lw["wt_f"], lw["b_f"], shared["cache_kt"], shared["cache_vt"],
                                shared["cache_logf_t"], page_table, page0, t_p, nb_s, seq_s)
    h, eidx, gates, rank, counts = _mix(oa_p, oa_s, ob_p, ob_s, proj, x2d, lw["w_a"], lw["w_b"], lw["w_o"],
                                        lw["ln_mix_g"], lw["ln_mix_b"], lw["wr_hi"], lw["wr_lo"], lw["b_r"], alpha)
    t = x2d.shape[0]
    n_tiles = (t * TOP_K + N_EXPERTS * (MOE_TILE - 1)) // MOE_TILE + 1
    dest, tile_expert, n_used = _moe_plan(eidx, rank, counts, n_tiles, layer * N_EXPERTS)
    x_sorted = _dispatch(h, dest, n_tiles * MOE_TILE)
    y_sorted = _experts(x_sorted, tile_expert, n_used,
                        shared["w_gate_up"], shared["b_gate_up"], shared["w_down"], shared["b_down"])
    y = _combine(y_sorted, dest, gates, h, lw["ln_ffn_g"], lw["ln_ffn_b"], alpha)
    return y, (kbuf, vbuf), kv_s, logft_p, logft_s, s_p, s_s


def kernel(x_prompt, x_sample, cache_k, cache_v, cache_logf, state_hgrn, page_table, w_in, b_fox_f, lb_logits,
           hgrn_norm_g, w_branch_a, w_branch_b, w_out, ln_mix_g, ln_mix_b, w_router, b_router, w_gate_up,
           b_gate_up, w_down, b_down, ln_ffn_g, ln_ffn_b):
    depth = w_in.shape[0]
    nb_p, seq_p, d = x_prompt.shape
    nb_s, seq_s, _ = x_sample.shape
    t_p, t_s = nb_p * seq_p, nb_s * seq_s
    alpha = (2 * depth) ** 0.25
    n_pool, page = cache_k.shape[1], cache_k.shape[2]
    dff2 = w_gate_up.shape[-1]

    pl_soft = jax.nn.softmax(lb_logits.astype(F32), axis=0)
    lower_bounds = jnp.cumsum(pl_soft, axis=0) - pl_soft[0:1]

    x2d = jnp.concatenate([x_prompt.reshape(t_p, d), x_sample.reshape(t_s, d)], axis=0)
    shared = {
        "page_table": page_table, "n_pool": n_pool,
        "cache_kt": cache_k.transpose(0, 1, 3, 4, 2).reshape(depth * n_pool, D_MODEL, page),
        "cache_vt": cache_v.transpose(0, 1, 3, 4, 2).reshape(depth * n_pool, D_MODEL, page),
        "cache_logf_t": cache_logf.transpose(0, 1, 3, 2).reshape(depth * n_pool, B_HEADS, page),
        "state": state_hgrn.reshape(depth * nb_s, A_HEADS, A_DK, A_DK),
        "w_gate_up": w_gate_up.reshape(depth * N_EXPERTS, d, dff2),
        "b_gate_up": b_gate_up.reshape(depth * N_EXPERTS, 1, dff2),
        "w_down": w_down.reshape(depth * N_EXPERTS, dff2 // 2, d),
        "b_down": b_down.reshape(depth * N_EXPERTS, 1, d),
    }
    wt_in = w_in.transpose(0, 2, 1)
    c_k = 5 * D_MODEL
    c_f = 7 * D_MODEL
    pad_r = LANES - N_EXPERTS
    kv_prev = None
    fp, sp, ks, vs, fs, ss = [], [], [], [], [], []
    for l in range(depth):
        wt = wt_in[l]
        wr_t = jnp.pad(w_router[l].T, ((0, pad_r), (0, 0)))
        wr_hi = wr_t.astype(BF16)
        lw = {
            "wt_main": jnp.concatenate([wt[:c_k], wt[c_f + B_HEADS:]], axis=0).astype(BF16),
            "wt_kv": wt[c_k:c_f].astype(BF16),
            "wt_f": wt[c_f:c_f + B_HEADS].astype(BF16),
            "b_f": b_fox_f[l],
            "lb": lower_bounds[l].reshape(1, d),
            "hgrn_g": hgrn_norm_g[l].reshape(1, A_DK),
            "w_a": w_branch_a[l].astype(BF16), "w_b": w_branch_b[l].astype(BF16), "w_o": w_out[l].astype(BF16),
            "ln_mix_g": ln_mix_g[l].reshape(1, d), "ln_mix_b": ln_mix_b[l].reshape(1, d),
            "ln_ffn_g": ln_ffn_g[l].reshape(1, d), "ln_ffn_b": ln_ffn_b[l].reshape(1, d),
            "b_r": jnp.pad(b_router[l], (0, pad_r), constant_values=MASK_VALUE).reshape(1, LANES),
            "wr_hi": wr_hi, "wr_lo": (wr_t - wr_hi.astype(F32)).astype(BF16),
        }
        x2d, kv_prev, kv_s, logft_p, logft_s, s_p, s_s = _layer(
            x2d, lw, shared, l, depth, (nb_p, seq_p, nb_s, seq_s), alpha, kv_prev)
        ks.append(kv_s[:, :D_MODEL].reshape(nb_s, seq_s, B_HEADS, B_DH))
        vs.append(kv_s[:, D_MODEL:].reshape(nb_s, seq_s, B_HEADS, B_DH))
        fp.append(logft_p)
        fs.append(logft_s)
        sp.append(s_p)
        ss.append(s_s)
    y_p = x2d[:t_p].reshape(nb_p, seq_p, d)
    y_s = x2d[t_p:].reshape(nb_s, seq_s, d)
    kbuf, vbuf = kv_prev
    k_p = kbuf.reshape(depth, nb_p, B_HEADS, B_DH, seq_p).transpose(0, 1, 4, 2, 3)
    v_p = vbuf.reshape(depth, nb_p, B_HEADS, B_DH, seq_p).transpose(0, 1, 4, 2, 3)
    f_p = jnp.stack(fp).transpose(0, 1, 3, 2)
    f_s = jnp.stack(fs).transpose(0, 1, 3, 2)
    return (y_p, y_s, k_p, v_p, f_p, jnp.stack(sp), jnp.stack(ks), jnp.stack(vs), f_s, jnp.stack(ss))
```

```python
import functools

import jax
import jax.numpy as jnp
from jax import lax
from jax.experimental import pallas as pl
from jax.experimental.pallas import tpu as pltpu

F32 = jnp.float32
BF16 = jnp.bfloat16

D_MODEL = 1024
A_HEADS = 8
A_DK = 128
B_HEADS = 16
B_DH = 64
N_EXPERTS = 32
TOP_K = 4
SWIGLU_LIMIT = 7.0
SWIGLU_ALPHA = 1.702
LN_EPS = 1e-5
RMS_EPS = 1e-6
MASK_VALUE = -1e30
LANES = 128
HGRN_CHUNK = 64
HGRN_SAFE_DECAY = 80.0
VMEM_LIMIT = 56 * 1024 * 1024
PROJ_Q0 = 4 * D_MODEL
PROJ_G0 = 5 * D_MODEL
MOE_TILE = 256


def _pick_tile(n, candidates):
    for c in candidates:
        if n % c == 0:
            return c
    return n


def _cparams(sem, vmem=None):
    return pltpu.CompilerParams(dimension_semantics=sem, vmem_limit_bytes=vmem or VMEM_LIMIT)


def _split3(x):
    hi = x.astype(BF16)
    r = x - hi.astype(F32)
    mid = r.astype(BF16)
    lo = (r - mid.astype(F32)).astype(BF16)
    return hi, mid, lo


def _dot(a, b):
    return jnp.dot(a, b, preferred_element_type=F32)


def _dot_nt(a, b):
    return lax.dot_general(a, b, (((1,), (1,)), ((), ())), preferred_element_type=F32)


def _dot_tn(a, b):
    return lax.dot_general(a, b, (((0,), (0,)), ((), ())), preferred_element_type=F32)


def _tri(n, upper=False):
    r = lax.broadcasted_iota(jnp.int32, (n, n), 0)
    c = lax.broadcasted_iota(jnp.int32, (n, n), 1)
    return ((r <= c) if upper else (c <= r)).astype(BF16)


def _cumsum_rows(x):
    tri = _tri(x.shape[0])
    hi, mid, lo = _split3(x)
    return _dot(tri, hi) + _dot(tri, mid) + _dot(tri, lo)


def _cumsum_lanes(x):
    tri = _tri(x.shape[1], upper=True)
    hi, mid, lo = _split3(x)
    return _dot(hi, tri) + _dot(mid, tri) + _dot(lo, tri)


def _log_sigmoid(z):
    return jnp.minimum(z, 0.0) - jnp.log(1.0 + jnp.exp(-jnp.abs(z)))


def _sigmoid(z):
    return 1.0 / (1.0 + jnp.exp(-z))


def _silu(z):
    return z * _sigmoid(z)


def _layer_norm(x, g, b):
    mu = jnp.mean(x, axis=-1, keepdims=True)
    xc = x - mu
    var = jnp.mean(xc * xc, axis=-1, keepdims=True)
    return xc * lax.rsqrt(var + LN_EPS) * g + b


def _drop_ref(kern, idx):
    def wrapped(*refs):
        return kern(*(refs[:idx] + refs[idx + 1:]))
    return wrapped


def _proj_kernel(x_ref, wt_ref, o_ref):
    o_ref[...] = _dot_nt(x_ref[...].astype(BF16), wt_ref[...])


def _project(x2d, wt, row0=0, rows=None):
    d = x2d.shape[1]
    t = x2d.shape[0] if rows is None else rows
    n = wt.shape[0]
    tm = _pick_tile(t, (1280, 640, 512, 256))
    tn = _pick_tile(n, (1024, 512, 256, 128))
    rb0 = row0 // tm
    return pl.pallas_call(
        _proj_kernel,
        grid=(n // tn, t // tm),
        in_specs=[pl.BlockSpec((tm, d), lambda j, i: (rb0 + i, 0)),
                  pl.BlockSpec((tn, d), lambda j, i: (j, 0))],
        out_specs=pl.BlockSpec((tm, tn), lambda j, i: (i, j)),
        out_shape=jax.ShapeDtypeStruct((t, n), F32),
        compiler_params=_cparams(("parallel", "parallel")),
        name="in_proj",
    )(x2d, wt)


def _kv_prompt_kernel(x_ref, wt_ref, k_ref, v_ref, *, layer, fill_others):
    def compute():
        kv = _dot_nt(wt_ref[...], x_ref[...].astype(BF16))
        k_ref[0, 0] = kv[:D_MODEL]
        v_ref[0, 0] = kv[D_MODEL:]

    if not fill_others:
        compute()
        return
    s = pl.program_id(0)
    pl.when(s == layer)(compute)

    @pl.when(s != layer)
    def _():
        k_ref[...] = jnp.zeros_like(k_ref)
        v_ref[...] = jnp.zeros_like(v_ref)


def _kv_prompt(x2d, wt_kv, layer, depth, nb, seq, prev):
    d = x2d.shape[1]
    tl = _pick_tile(seq, (512, 256, 128))
    nl = seq // tl
    first = prev is None
    in_specs = [pl.BlockSpec((tl, d), lambda s, b, i: (b * nl + i, 0)),
                pl.BlockSpec((2 * D_MODEL, d), lambda s, b, i: (0, 0))]
    args = [x2d, wt_kv]
    kern = functools.partial(_kv_prompt_kernel, layer=layer, fill_others=first)
    aliases = {}
    if first:
        ospec = pl.BlockSpec((1, 1, D_MODEL, tl), lambda s, b, i: (s, b, 0, i))
    else:
        in_specs += [pl.BlockSpec(memory_space=pl.ANY)] * 2
        args += list(prev)
        aliases = {2: 0, 3: 1}
        kern = _drop_ref(_drop_ref(kern, 2), 3)
        ospec = pl.BlockSpec((1, 1, D_MODEL, tl), lambda s, b, i: (layer, b, 0, i))
    out = jax.ShapeDtypeStruct((depth, nb, D_MODEL, seq), F32)
    return pl.pallas_call(
        kern,
        grid=(depth if first else 1, nb, nl),
        in_specs=in_specs,
        out_specs=[ospec, ospec],
        out_shape=[out, out],
        input_output_aliases=aliases,
        compiler_params=_cparams(("parallel", "parallel", "parallel")),
        name="kv_prompt",
    )(*args)


def _fox_gate_prompt_kernel(x_ref, wft_ref, bf_ref, bft_ref, logft_ref, cum_ref, cumt_ref, carry_ref, carryt_ref):
    i = pl.program_id(1)

    @pl.when(i == 0)
    def _():
        carry_ref[...] = jnp.zeros_like(carry_ref)
        carryt_ref[...] = jnp.zeros_like(carryt_ref)

    xb = x_ref[...].astype(BF16)
    logf = _log_sigmoid(_dot_nt(xb, wft_ref[...]) + bf_ref[...])
    logft = _log_sigmoid(_dot_nt(wft_ref[...], xb) + bft_ref[...])
    tl = logf.shape[0]
    cs = _cumsum_rows(logf) + carry_ref[...]
    cst = _cumsum_lanes(logft) + carryt_ref[...]
    carry_ref[...] = cs[tl - 1:tl, :]
    carryt_ref[...] = cst[:, tl - 1:tl]
    logft_ref[0] = logft
    cum_ref[0] = cs
    cumt_ref[0] = cst


def _fox_gate_prompt(x2d, wft, bf, nb, seq):
    d = x2d.shape[1]
    tl = _pick_tile(seq, (256, 128))
    nl = seq // tl
    out_t = jax.ShapeDtypeStruct((nb, B_HEADS, seq), F32)
    tspec = pl.BlockSpec((1, B_HEADS, tl), lambda b, i: (b, 0, i))
    return pl.pallas_call(
        _fox_gate_prompt_kernel,
        grid=(nb, nl),
        in_specs=[pl.BlockSpec((tl, d), lambda b, i: (b * nl + i, 0)),
                  pl.BlockSpec((B_HEADS, d), lambda b, i: (0, 0)),
                  pl.BlockSpec((1, B_HEADS), lambda b, i: (0, 0)),
                  pl.BlockSpec((B_HEADS, 1), lambda b, i: (0, 0))],
        out_specs=[tspec, pl.BlockSpec((1, tl, B_HEADS), lambda b, i: (b, i, 0)), tspec],
        out_shape=[out_t, jax.ShapeDtypeStruct((nb, seq, B_HEADS), F32), out_t],
        scratch_shapes=[pltpu.VMEM((1, B_HEADS), F32), pltpu.VMEM((B_HEADS, 1), F32)],
        compiler_params=_cparams(("parallel", "arbitrary")),
        name="fox_gate_prompt",
    )(x2d, wft, bf.reshape(1, B_HEADS), bf.reshape(B_HEADS, 1))


def _hgrn_kernel(*refs, chunk, n_chunks, has_s0):
    if has_s0:
        (aq_ref, af_ref, ai_ref, ag_ref, lb_ref, g_ref, s0_ref, o_ref, snew_ref,
         st_ref, a_ref, gs_ref, ks_ref, qt_ref, kt_ref, vb_ref) = refs
    else:
        (aq_ref, af_ref, ai_ref, ag_ref, lb_ref, g_ref, o_ref, snew_ref,
         st_ref, a_ref, gs_ref, ks_ref, qt_ref, kt_ref, vb_ref) = refs
        s0_ref = None
    i = pl.program_id(1)
    c = chunk
    rows = c * n_chunks

    @pl.when(i == 0)
    def _():
        for h in range(A_HEADS):
            if has_s0:
                st_ref[h] = s0_ref[0, h].T
            else:
                st_ref[h] = jnp.zeros((A_DK, A_DK), F32)

    lbv = lb_ref[...]
    gv = g_ref[...]
    scale = A_DK ** -0.5

    z = af_ref[...]
    logf = jnp.log(lbv + (1.0 - lbv) * _sigmoid(z))
    kk = (1.0 - lbv) * _sigmoid(-z)
    rr = lax.broadcasted_iota(jnp.int32, (rows, rows), 0)
    cc = lax.broadcasted_iota(jnp.int32, (rows, rows), 1)
    tri = ((rr // c == cc // c) & (cc <= rr)).astype(BF16)
    hi, mid, lo = _split3(logf)
    g_cum = _dot(tri, hi) + _dot(tri, mid) + _dot(tri, lo)
    gs_ref[...] = g_cum
    ks_ref[...] = kk
    qt_ref[...] = (_silu(aq_ref[...]) * scale * jnp.exp(g_cum)).astype(BF16)
    kt_ref[...] = (kk * jnp.exp(-g_cum)).astype(BF16)
    vb_ref[...] = ai_ref[...].astype(BF16)
    a_ref[...] = jnp.zeros_like(a_ref)

    row = lax.broadcasted_iota(jnp.int32, (c, c), 0)
    col = lax.broadcasted_iota(jnp.int32, (c, c), 1)
    causal = col <= row

    safes = []
    for ci in range(n_chunks):
        r = slice(ci * c, (ci + 1) * c)
        safe = jnp.min(gs_ref[(ci + 1) * c - 1:(ci + 1) * c, :]) >= -HGRN_SAFE_DECAY
        safes.append(safe)

        @pl.when(jnp.logical_not(safe))
        def _():
            q = _silu(aq_ref[r, :]) * scale
            g_c = gs_ref[r, :]

            def col_j(j, carry):
                gj = gs_ref[pl.ds(ci * c + j, 1), :]
                kj = ks_ref[pl.ds(ci * c + j, 1), :]
                w = q * kj * jnp.exp(jnp.minimum(g_c - gj, 0.0))
                for h in range(A_HEADS):
                    hs = slice(h * A_DK, (h + 1) * A_DK)
                    cj = jnp.sum(w[:, hs], axis=1, keepdims=True)
                    a_ref[ci, h] = a_ref[ci, h] + jnp.where(col == j, cj, 0.0)
                return carry

            lax.fori_loop(0, c, col_j, 0)

    for ci in range(n_chunks):
        r = slice(ci * c, (ci + 1) * c)
        safe = safes[ci]
        g_c = gs_ref[r, :]
        gend = g_c[c - 1:c, :]
        khat = (ks_ref[r, :] * jnp.exp(gend - g_c)).astype(BF16)
        egend = jnp.exp(gend)
        for h in range(A_HEADS):
            hs = slice(h * A_DK, (h + 1) * A_DK)
            qt = qt_ref[r, hs]
            vb = vb_ref[r, hs]
            a_fast = _dot_nt(qt, kt_ref[r, hs])
            att = jnp.where(causal, jnp.where(safe, a_fast, a_ref[ci, h]), 0.0).astype(BF16)
            st = st_ref[h]
            o = _dot(att, vb) + _dot_nt(qt, st.astype(BF16))
            st_ref[h] = st * egend[:, hs] + _dot_tn(vb, khat[:, hs])
            ms = jnp.mean(o * o, axis=-1, keepdims=True)
            o = o * lax.rsqrt(ms + RMS_EPS) * gv
            o_ref[r, hs] = o * _silu(ag_ref[r, hs])

    @pl.when(i == pl.num_programs(1) - 1)
    def _():
        for h in range(A_HEADS):
            snew_ref[0, h] = st_ref[h].T


def _hgrn(proj, lb, g, s0, s0_base, row0, nb, seq):
    chunk = min(HGRN_CHUNK, seq)
    lb_rows = _pick_tile(seq, (256, chunk))
    nl = seq // lb_rows
    rb0 = row0 // lb_rows
    has_s0 = s0 is not None
    in_specs = [pl.BlockSpec((lb_rows, D_MODEL), functools.partial(lambda b, i, k: (rb0 + b * nl + i, k), k=k))
                for k in range(4)]
    in_specs += [pl.BlockSpec((1, D_MODEL), lambda b, i: (0, 0)),
                 pl.BlockSpec((1, A_DK), lambda b, i: (0, 0))]
    args = [proj, proj, proj, proj, lb, g]
    if has_s0:
        in_specs.append(pl.BlockSpec((1, A_HEADS, A_DK, A_DK), lambda b, i: (s0_base + b, 0, 0, 0)))
        args.append(s0)
    kern = functools.partial(_hgrn_kernel, chunk=chunk, n_chunks=lb_rows // chunk, has_s0=has_s0)
    return pl.pallas_call(
        kern,
        grid=(nb, nl),
        in_specs=in_specs,
        out_specs=[pl.BlockSpec((lb_rows, D_MODEL), lambda b, i: (b * nl + i, 0)),
                   pl.BlockSpec((1, A_HEADS, A_DK, A_DK), lambda b, i: (b, 0, 0, 0))],
        out_shape=[jax.ShapeDtypeStruct((nb * seq, D_MODEL), F32),
                   jax.ShapeDtypeStruct((nb, A_HEADS, A_DK, A_DK), F32)],
        scratch_shapes=[pltpu.VMEM((A_HEADS, A_DK, A_DK), F32),
                        pltpu.VMEM((lb_rows // chunk, A_HEADS, chunk, chunk), F32),
                        pltpu.VMEM((lb_rows, D_MODEL), F32), pltpu.VMEM((lb_rows, D_MODEL), F32),
                        pltpu.VMEM((lb_rows, D_MODEL), BF16), pltpu.VMEM((lb_rows, D_MODEL), BF16),
                        pltpu.VMEM((lb_rows, D_MODEL), BF16)],
        compiler_params=_cparams(("parallel", "arbitrary")),
        name="hgrn",
    )(*args)


def _fox_prompt_kernel(q_ref, k_ref, v_ref, cum_ref, cumt_ref, o_ref, kn_ref, qt_ref, vt_ref, s_ref, p_ref,
                       *, tq, tk):
    j = pl.program_id(1)
    seq = q_ref.shape[0]
    nq = seq // tq
    hpair = (2 * j, 2 * j + 1)

    kn_ref[:, :LANES] = k_ref[0, 0].T.astype(BF16)
    h16 = lax.broadcasted_iota(jnp.int32, (B_HEADS, LANES), 0)
    l16 = lax.broadcasted_iota(jnp.int32, (B_HEADS, LANES), 1)
    kaug = jnp.zeros((seq, LANES), F32)
    for part, term in enumerate(_split3(cum_ref[0])):
        sel = jnp.where(((h16 == hpair[0]) & (l16 == part)) | ((h16 == hpair[1]) & (l16 == 3 + part)), -1.0, 0.0)
        kaug = kaug + _dot(term, sel.astype(BF16))
    lane_s = lax.broadcasted_iota(jnp.int32, (seq, LANES), 1)
    kn_ref[:, LANES:] = (kaug + jnp.where((lane_s >= 6) & (lane_s < 9), 1.0, 0.0)).astype(BF16)

    qt = (q_ref[...] * (B_DH ** -0.5)).T
    row = lax.broadcasted_iota(jnp.int32, (LANES, seq), 0)
    r16 = lax.broadcasted_iota(jnp.int32, (LANES, B_HEADS), 0)
    c16 = lax.broadcasted_iota(jnp.int32, (LANES, B_HEADS), 1)
    cumt_terms = _split3(cumt_ref[0])
    for h in range(2):
        qt_ref[h, :LANES, :] = jnp.where(row // B_DH == h, qt, 0.0).astype(BF16)
        qaug = jnp.where((row >= 3 * h) & (row < 3 * h + 3), 1.0, 0.0)
        for part, term in enumerate(cumt_terms):
            sel = jnp.where((r16 == 6 + part) & (c16 == hpair[h]), 1.0, 0.0)
            qaug = qaug + _dot(sel.astype(BF16), term)
        qt_ref[h, LANES:, :] = qaug.astype(BF16)
    vt_ref[...] = v_ref[0, 0].astype(BF16)

    kpos = lax.broadcasted_iota(jnp.int32, (tk, tq), 0)
    qpos = lax.broadcasted_iota(jnp.int32, (tk, tq), 1)
    kpq = tq // tk

    sub = tk // 8

    def fold(x):
        return x.reshape(sub, 8, tq)

    for qi in range(nq):
        q0 = qi * tq
        nfull = qi * kpq
        outs = []
        for h in range(2):
            qa = qt_ref[h, :, q0:q0 + tq]

            def scores(ki, mx, qa=qa, h=h):
                ks = pl.ds(pl.multiple_of(ki * tk, tk), tk)
                s = _dot(kn_ref[ks, :], qa)
                s_ref[h, ks, :] = s
                return jnp.maximum(mx, jnp.max(fold(s), axis=0))

            mx = jnp.full((8, tq), -jnp.inf, F32)
            if nfull:
                mx = lax.fori_loop(0, nfull, scores, mx, unroll=2)
            for dk in range(kpq):
                k0 = (nfull + dk) * tk
                s = _dot(kn_ref[k0:k0 + tk, :], qa)
                s = jnp.where(k0 + kpos <= q0 + qpos, s, MASK_VALUE)
                s_ref[h, k0:k0 + tk, :] = s
                mx = jnp.maximum(mx, jnp.max(fold(s), axis=0))
            m = jnp.max(mx, axis=0, keepdims=True)

            def probs(ki, lsum, m=m, h=h):
                ks = pl.ds(pl.multiple_of(ki * tk, tk), tk)
                p = jnp.exp(s_ref[h, ks, :] - m)
                p_ref[h, ks, :] = p.astype(BF16)
                return lsum + jnp.sum(fold(p), axis=0)

            nk = nfull + kpq
            lsum = lax.fori_loop(0, nk, probs, jnp.zeros((8, tq), F32), unroll=2)
            l = jnp.sum(lsum, axis=0, keepdims=True)
            acc = _dot(vt_ref[h * B_DH:(h + 1) * B_DH, 0:nk * tk], p_ref[h, 0:nk * tk, :])
            outs.append(acc * (1.0 / l))
        o_ref[q0:q0 + tq, :] = jnp.concatenate(outs, axis=0).T


def _fox_prompt(proj, kbuf, vbuf, layer, cum, cum_t, nb, seq):
    t = nb * seq
    tq = _pick_tile(seq, (512, 256, 128))
    tk = min(tq, 256)
    npair = B_HEADS // 2
    c0 = PROJ_Q0 // LANES
    kvspec = pl.BlockSpec((1, 1, LANES, seq), lambda b, j: (layer, b, j, 0))
    return pl.pallas_call(
        functools.partial(_fox_prompt_kernel, tq=tq, tk=tk),
        grid=(nb, npair),
        in_specs=[pl.BlockSpec((seq, LANES), lambda b, j: (b, c0 + j)), kvspec, kvspec,
                  pl.BlockSpec((1, seq, B_HEADS), lambda b, j: (b, 0, 0)),
                  pl.BlockSpec((1, B_HEADS, seq), lambda b, j: (b, 0, 0))],
        out_specs=pl.BlockSpec((seq, LANES), lambda b, j: (b, j)),
        out_shape=jax.ShapeDtypeStruct((t, D_MODEL), F32),
        scratch_shapes=[pltpu.VMEM((seq, 2 * LANES), BF16), pltpu.VMEM((2, 2 * LANES, seq), BF16),
                        pltpu.VMEM((LANES, seq), BF16), pltpu.VMEM((2, seq, tq), F32),
                        pltpu.VMEM((2, seq, tq), BF16)],
        compiler_params=_cparams(("parallel", "parallel")),
        name="fox_prompt",
    )(proj, kbuf, vbuf, cum, cum_t)


def _fox_sample_kernel(*refs, pages_per_step, ns):
    pps = pages_per_step
    q_ref, kvn_ref, x_ref, wft_ref, bft_ref = refs[1:6]
    k_refs = refs[6:6 + pps]
    v_refs = refs[6 + pps:6 + 2 * pps]
    f_refs = refs[6 + 2 * pps:6 + 3 * pps]
    o_ref, logft_ref = refs[6 + 3 * pps:8 + 3 * pps]
    qbd_ref, fq_ref, cn_ref, past_ref, m_ref, l_ref, acc_ref = refs[8 + 3 * pps:]
    g = pl.program_id(1)
    nrow = B_HEADS * ns
    rh = lax.broadcasted_iota(jnp.int32, (nrow, D_MODEL), 0) // ns
    ch = lax.broadcasted_iota(jnp.int32, (nrow, D_MODEL), 1) // B_DH

    def expand_rows(x):
        n = x.shape[1]
        return jnp.broadcast_to(x[:, None, :], (B_HEADS, ns, n)).reshape(nrow, n)

    @pl.when(g == 0)
    def _():
        q = q_ref[...] * (B_DH ** -0.5)
        qrep = jnp.broadcast_to(q[None], (B_HEADS, ns, D_MODEL)).reshape(nrow, D_MODEL)
        qbd_ref[...] = jnp.where(rh == ch, qrep, 0.0).astype(BF16)
        logft = _log_sigmoid(_dot_nt(wft_ref[...], x_ref[...].astype(BF16)) + bft_ref[...])
        logft_ref[0] = logft
        cn = _cumsum_lanes(logft)
        cn_ref[...] = cn
        tsel = (lax.broadcasted_iota(jnp.int32, (nrow, ns), 0) % ns
                == lax.broadcasted_iota(jnp.int32, (nrow, ns), 1))
        fq_ref[...] = jnp.sum(jnp.where(tsel, expand_rows(cn), 0.0), axis=1, keepdims=True)
        past_ref[...] = jnp.zeros_like(past_ref)
        m_ref[...] = jnp.full_like(m_ref, -jnp.inf)
        l_ref[...] = jnp.zeros_like(l_ref)
        acc_ref[...] = jnp.zeros_like(acc_ref)

    def update(s, pv):
        m_old = m_ref[...]
        m_new = jnp.maximum(m_old, jnp.max(s, axis=1, keepdims=True))
        alpha = jnp.exp(m_old - m_new)
        p = jnp.exp(s - m_new)
        l_ref[...] = alpha * l_ref[...] + jnp.sum(p, axis=1, keepdims=True)
        acc_ref[...] = alpha * acc_ref[...] + pv(p.astype(BF16))
        m_ref[...] = m_new

    page = k_refs[0].shape[2]
    lf = jnp.concatenate([f_refs[pi][0] for pi in range(pps)], axis=0)
    cs = _cumsum_lanes(lf)
    offset = past_ref[...]
    s_parts = []
    for pi in range(pps):
        cs_p = cs[pi * B_HEADS:(pi + 1) * B_HEADS, :] + offset
        offset = cs_p[:, page - 1:]
        kt = k_refs[pi][0].astype(BF16)
        s_parts.append(_dot(qbd_ref[...], kt) + fq_ref[...] - expand_rows(cs_p))
    past_ref[...] = offset

    def pv_pages(p):
        out = None
        for pi in range(pps):
            term = _dot_nt(p[:, pi * page:(pi + 1) * page], v_refs[pi][0].astype(BF16))
            out = term if out is None else out + term
        return out

    update(jnp.concatenate(s_parts, axis=1), pv_pages)

    @pl.when(g == pl.num_programs(1) - 1)
    def _():
        kb = kvn_ref[:, :D_MODEL].astype(BF16)
        vb = kvn_ref[:, D_MODEL:].astype(BF16)
        fk = expand_rows(cn_ref[...] + past_ref[...])
        s = _dot_nt(qbd_ref[...], kb) + fq_ref[...] - fk
        tq = lax.broadcasted_iota(jnp.int32, (nrow, ns), 0) % ns
        tk = lax.broadcasted_iota(jnp.int32, (nrow, ns), 1)
        s = jnp.where(tq >= tk, s, MASK_VALUE)
        update(s, lambda p: _dot(p, vb))
        o = acc_ref[...] / l_ref[...]
        for h in range(B_HEADS):
            o_ref[:, h * B_DH:(h + 1) * B_DH] = o[h * ns:(h + 1) * ns, h * B_DH:(h + 1) * B_DH]


def _fox_sample(proj, kv_new, x2d, wft, bf, cache_kt, cache_vt, cache_logf_t, page_table, page0, row0, nb, ns):
    d = x2d.shape[1]
    n_pages = page_table.shape[1]
    page = cache_kt.shape[2]
    pps = _pick_tile(n_pages, (8, 4, 2, 1))
    ng = n_pages // pps
    rb0 = row0 // ns
    nrow = B_HEADS * ns

    def page_spec(pi, rows):
        return pl.BlockSpec((1, rows, page), lambda b, g, pt: (page0 + pt[b, g * pps + pi], 0, 0))

    in_specs = [pl.BlockSpec((ns, D_MODEL), lambda b, g, pt: (rb0 + b, PROJ_Q0 // D_MODEL)),
                pl.BlockSpec((ns, 2 * D_MODEL), lambda b, g, pt: (b, 0)),
                pl.BlockSpec((ns, d), lambda b, g, pt: (rb0 + b, 0)),
                pl.BlockSpec((B_HEADS, d), lambda b, g, pt: (0, 0)),
                pl.BlockSpec((B_HEADS, 1), lambda b, g, pt: (0, 0))]
    in_specs += [page_spec(pi, D_MODEL) for pi in range(pps)] * 2
    in_specs += [page_spec(pi, B_HEADS) for pi in range(pps)]
    args = [proj, kv_new, x2d, wft, bf.reshape(B_HEADS, 1)]
    args += [cache_kt] * pps + [cache_vt] * pps + [cache_logf_t] * pps
    grid_spec = pltpu.PrefetchScalarGridSpec(
        num_scalar_prefetch=1,
        grid=(nb, ng),
        in_specs=in_specs,
        out_specs=[pl.BlockSpec((ns, D_MODEL), lambda b, g, pt: (b, 0)),
                   pl.BlockSpec((1, B_HEADS, ns), lambda b, g, pt: (b, 0, 0))],
        scratch_shapes=[pltpu.VMEM((nrow, D_MODEL), BF16), pltpu.VMEM((nrow, 1), F32),
                        pltpu.VMEM((B_HEADS, ns), F32), pltpu.VMEM((B_HEADS, 1), F32),
                        pltpu.VMEM((nrow, 1), F32), pltpu.VMEM((nrow, 1), F32),
                        pltpu.VMEM((nrow, D_MODEL), F32)],
    )
    return pl.pallas_call(
        functools.partial(_fox_sample_kernel, pages_per_step=pps, ns=ns),
        grid_spec=grid_spec,
        out_shape=[jax.ShapeDtypeStruct((nb * ns, D_MODEL), F32),
                   jax.ShapeDtypeStruct((nb, B_HEADS, ns), F32)],
        compiler_params=_cparams(("parallel", "arbitrary")),
        name="fox_sample",
    )(page_table, *args)


def _mix_kernel(oap_ref, oas_ref, obp_ref, obs_ref, ga_ref, gb_ref, x_ref, wa_ref, wb_ref, wo_ref, lng_ref,
                lnb_ref, wrh_ref, wrl_ref, br_ref, h_ref, eidx_ref, gate_ref, rank_ref, cnt_ref, carry_ref,
                *, alpha, n_prompt_tiles):
    i = pl.program_id(0)

    @pl.when(i == 0)
    def _():
        carry_ref[...] = jnp.zeros_like(carry_ref)

    is_prompt = i < n_prompt_tiles
    oa = jnp.where(is_prompt, oap_ref[...], oas_ref[...])
    ob = jnp.where(is_prompt, obp_ref[...], obs_ref[...])
    ya = _dot(oa.astype(BF16), wa_ref[...])
    yb = _dot(ob.astype(BF16), wb_ref[...])
    m = _sigmoid(ga_ref[...]) * ya + _sigmoid(gb_ref[...]) * yb
    mix = _dot(m.astype(BF16), wo_ref[...])
    h = _layer_norm(alpha * x_ref[...] + mix, lng_ref[...], lnb_ref[...])
    h_ref[...] = h

    h_hi = h.astype(BF16)
    h_lo = (h - h_hi.astype(F32)).astype(BF16)
    logits = (_dot_nt(h_hi, wrh_ref[...]) + _dot_nt(h_lo, wrh_ref[...]) + _dot_nt(h_hi, wrl_ref[...])
              + br_ref[...])
    tm = logits.shape[0]
    lane = lax.broadcasted_iota(jnp.int32, (tm, LANES), 1)
    work = logits
    idxs, vals = [], []
    for _ in range(TOP_K):
        mx = jnp.max(work, axis=1, keepdims=True)
        ix = jnp.min(jnp.where(work == mx, lane, LANES), axis=1, keepdims=True)
        idxs.append(ix)
        vals.append(mx)
        work = jnp.where(lane == ix, -jnp.inf, work)
    exps = [jnp.exp(v - vals[0]) for v in vals]
    denom = exps[0] + exps[1] + exps[2] + exps[3]
    multihot = jnp.zeros((tm, LANES), F32)
    for ix in idxs:
        multihot = multihot + jnp.where(lane == ix, 1.0, 0.0)
    r = lax.broadcasted_iota(jnp.int32, (tm, tm), 0)
    c = lax.broadcasted_iota(jnp.int32, (tm, tm), 1)
    before = _dot((c < r).astype(BF16), multihot.astype(BF16)) + carry_ref[...]
    carry_ref[...] = carry_ref[...] + jnp.sum(multihot, axis=0, keepdims=True)
    col4 = lax.broadcasted_iota(jnp.int32, (tm, TOP_K), 1)
    eidx = jnp.zeros((tm, TOP_K), jnp.int32)
    gates = jnp.zeros((tm, TOP_K), F32)
    ranks = jnp.zeros((tm, TOP_K), F32)
    for k in range(TOP_K):
        rk = jnp.sum(jnp.where(lane == idxs[k], before, 0.0), axis=1, keepdims=True)
        eidx = jnp.where(col4 == k, idxs[k], eidx)
        gates = jnp.where(col4 == k, exps[k] / denom, gates)
        ranks = jnp.where(col4 == k, rk, ranks)
    eidx_ref[...] = eidx
    gate_ref[...] = gates
    rank_ref[...] = ranks.astype(jnp.int32)
    cnt_ref[...] = carry_ref[...].astype(jnp.int32)


def _mix(oa_p, oa_s, ob_p, ob_s, proj, x2d, wa, wb, wo, lng, lnb, wrh, wrl, br, alpha):
    t = x2d.shape[0]
    t_p, t_s = oa_p.shape[0], oa_s.shape[0]
    tm = next(c for c in (256, 128, 64, 32, 16, 8) if t_p % c == 0 and t_s % c == 0)
    npt = t_p // tm
    row = lambda i: (i, 0)
    prow = lambda i: (jnp.minimum(i, npt - 1), 0)
    srow = lambda i: (jnp.maximum(i - npt, 0), 0)
    const = lambda i: (0, 0)
    wspec = pl.BlockSpec((D_MODEL, D_MODEL), const)
    vspec = pl.BlockSpec((1, D_MODEL), const)
    return pl.pallas_call(
        functools.partial(_mix_kernel, alpha=alpha, n_prompt_tiles=npt),
        grid=(t // tm,),
        in_specs=[pl.BlockSpec((tm, D_MODEL), prow), pl.BlockSpec((tm, D_MODEL), srow),
                  pl.BlockSpec((tm, D_MODEL), prow), pl.BlockSpec((tm, D_MODEL), srow),
                  pl.BlockSpec((tm, D_MODEL), lambda i: (i, PROJ_G0 // D_MODEL)),
                  pl.BlockSpec((tm, D_MODEL), lambda i: (i, PROJ_G0 // D_MODEL + 1)),
                  pl.BlockSpec((tm, D_MODEL), row), wspec, wspec, wspec, vspec, vspec,
                  pl.BlockSpec((LANES, D_MODEL), const), pl.BlockSpec((LANES, D_MODEL), const),
                  pl.BlockSpec((1, LANES), const)],
        out_specs=[pl.BlockSpec((tm, D_MODEL), row), pl.BlockSpec((tm, TOP_K), row),
                   pl.BlockSpec((tm, TOP_K), row), pl.BlockSpec((tm, TOP_K), row),
                   pl.BlockSpec((1, LANES), const)],
        out_shape=[jax.ShapeDtypeStruct((t, D_MODEL), F32), jax.ShapeDtypeStruct((t, TOP_K), jnp.int32),
                   jax.ShapeDtypeStruct((t, TOP_K), F32), jax.ShapeDtypeStruct((t, TOP_K), jnp.int32),
                   jax.ShapeDtypeStruct((1, LANES), jnp.int32)],
        scratch_shapes=[pltpu.VMEM((1, LANES), F32)],
        compiler_params=_cparams(("arbitrary",)),
        name="mix_router",
    )(oa_p, oa_s, ob_p, ob_s, proj, proj, x2d, wa, wb, wo, lng, lnb, wrh, wrl, br)


def _dispatch_kernel(dest_ref, h_ref, xz_hbm, x_hbm, stage, sems, *, tt):
    del xz_hbm
    i = pl.program_id(0)
    n = pl.num_programs(0)
    slot = i % 2
    stage[slot] = h_ref[...]

    def issue(r, carry):
        for k in range(TOP_K):
            row = dest_ref[(i * tt + r) * TOP_K + k]
            pltpu.make_async_copy(stage.at[slot, pl.ds(r, 1), :], x_hbm.at[pl.ds(row, 1), :],
                                  sems.at[slot]).start(priority=k % 2)
        return carry

    lax.fori_loop(0, tt, issue, 0, unroll=4)

    def wait_slot(s):
        for _ in range(TOP_K):
            pltpu.make_async_copy(stage.at[s], x_hbm.at[pl.ds(0, tt), :], sems.at[s]).wait()

    @pl.when(i > 0)
    def _():
        wait_slot(1 - slot)

    @pl.when(i == n - 1)
    def _():
        wait_slot(slot)


def _dispatch(h, dest, n_rows):
    t = h.shape[0]
    tt = _pick_tile(t, (256, 128, 64, 16, 8))
    grid_spec = pltpu.PrefetchScalarGridSpec(
        num_scalar_prefetch=1,
        grid=(t // tt,),
        in_specs=[pl.BlockSpec((tt, D_MODEL), lambda i, d: (i, 0)),
                  pl.BlockSpec(memory_space=pl.ANY)],
        out_specs=pl.BlockSpec(memory_space=pl.ANY),
        scratch_shapes=[pltpu.VMEM((2, tt, D_MODEL), F32), pltpu.SemaphoreType.DMA((2,))],
    )
    return pl.pallas_call(
        functools.partial(_dispatch_kernel, tt=tt),
        grid_spec=grid_spec,
        out_shape=jax.ShapeDtypeStruct((n_rows, D_MODEL), F32),
        input_output_aliases={2: 0},
        compiler_params=_cparams(("arbitrary",)),
        name="moe_dispatch",
    )(dest, h, jnp.zeros((n_rows, D_MODEL), F32))


def _expert_kernel(te_ref, nused_ref, x_ref, wgu_ref, bgu_ref, wd_ref, bd_ref, y_ref, wgu_bf, wd_bf):
    i = pl.program_id(0)
    used = i < nused_ref[0]
    prev = te_ref[jnp.maximum(i - 1, 0)]
    fresh = jnp.logical_or(i == 0, te_ref[i] != prev)

    @pl.when(jnp.logical_and(used, fresh))
    def _():
        wgu_bf[...] = wgu_ref[0].astype(BF16)
        wd_bf[...] = wd_ref[0].astype(BF16)

    @pl.when(used)
    def _():
        x = x_ref[...].astype(BF16)
        hcat = _dot(x, wgu_bf[...]) + bgu_ref[0]
        dff = hcat.shape[1] // 2
        gate = jnp.minimum(hcat[:, :dff], SWIGLU_LIMIT)
        up = jnp.clip(hcat[:, dff:], -SWIGLU_LIMIT, SWIGLU_LIMIT)
        act = (up + 1.0) * gate * _sigmoid(SWIGLU_ALPHA * gate)
        y_ref[...] = _dot(act.astype(BF16), wd_bf[...]) + bd_ref[0]

    @pl.when(jnp.logical_not(used))
    def _():
        y_ref[...] = jnp.zeros_like(y_ref)


def _experts(x_sorted, tile_expert, n_used, wgu, bgu, wd, bd):
    tm = MOE_TILE
    n_tiles = x_sorted.shape[0] // tm
    dff2 = wgu.shape[2]
    grid_spec = pltpu.PrefetchScalarGridSpec(
        num_scalar_prefetch=2,
        grid=(n_tiles,),
        in_specs=[pl.BlockSpec((tm, D_MODEL), lambda i, te, n: (i, 0)),
                  pl.BlockSpec((1, D_MODEL, dff2), lambda i, te, n: (te[i], 0, 0)),
                  pl.BlockSpec((1, 1, dff2), lambda i, te, n: (te[i], 0, 0)),
                  pl.BlockSpec((1, dff2 // 2, D_MODEL), lambda i, te, n: (te[i], 0, 0)),
                  pl.BlockSpec((1, 1, D_MODEL), lambda i, te, n: (te[i], 0, 0))],
        out_specs=pl.BlockSpec((tm, D_MODEL), lambda i, te, n: (i, 0)),
        scratch_shapes=[pltpu.VMEM((D_MODEL, dff2), BF16), pltpu.VMEM((dff2 // 2, D_MODEL), BF16)],
    )
    return pl.pallas_call(
        _expert_kernel,
        grid_spec=grid_spec,
        out_shape=jax.ShapeDtypeStruct((n_tiles * tm, D_MODEL), F32),
        compiler_params=_cparams(("arbitrary",)),
        name="moe_experts",
    )(tile_expert, n_used, x_sorted, wgu, bgu, wd, bd)


def _combine_kernel(dest_ref, y_hbm, gate_ref, h_ref, lng_ref, lnb_ref, o_ref, buf, sems, *, tt, alpha):
    i = pl.program_id(0)
    n = pl.num_programs(0)
    slot = i % 2

    def issue_tile(tile, s):
        def issue(r, carry):
            for k in range(TOP_K):
                row = dest_ref[(tile * tt + r) * TOP_K + k]
                pltpu.make_async_copy(y_hbm.at[pl.ds(row, 1), :], buf.at[s, k, pl.ds(r, 1), :],
                                      sems.at[s]).start(priority=k % 2)
            return carry

        lax.fori_loop(0, tt, issue, 0, unroll=4)

    @pl.when(i == 0)
    def _():
        issue_tile(0, 0)

    @pl.when(i + 1 < n)
    def _():
        issue_tile(i + 1, 1 - slot)

    for k in range(TOP_K):
        pltpu.make_async_copy(y_hbm.at[pl.ds(0, tt), :], buf.at[slot, k], sems.at[slot]).wait()
    gates = gate_ref[...]
    ffn = gates[:, 0:1] * buf[slot, 0]
    for k in range(1, TOP_K):
        ffn = ffn + gates[:, k:k + 1] * buf[slot, k]
    o_ref[...] = _layer_norm(alpha * h_ref[...] + ffn, lng_ref[...], lnb_ref[...])


def _combine(y_sorted, dest, gates, h, lng, lnb, alpha):
    t = h.shape[0]
    tt = _pick_tile(t, (128, 64, 16, 8))
    grid_spec = pltpu.PrefetchScalarGridSpec(
        num_scalar_prefetch=1,
        grid=(t // tt,),
        in_specs=[pl.BlockSpec(memory_space=pl.ANY),
                  pl.BlockSpec((tt, TOP_K), lambda i, d: (i, 0)),
                  pl.BlockSpec((tt, D_MODEL), lambda i, d: (i, 0)),
                  pl.BlockSpec((1, D_MODEL), lambda i, d: (0, 0)),
                  pl.BlockSpec((1, D_MODEL), lambda i, d: (0, 0))],
        out_specs=pl.BlockSpec((tt, D_MODEL), lambda i, d: (i, 0)),
        scratch_shapes=[pltpu.VMEM((2, TOP_K, tt, D_MODEL), F32), pltpu.SemaphoreType.DMA((2,))],
    )
    return pl.pallas_call(
        functools.partial(_combine_kernel, tt=tt, alpha=alpha),
        grid_spec=grid_spec,
        out_shape=jax.ShapeDtypeStruct((t, D_MODEL), F32),
        compiler_params=_cparams(("arbitrary",)),
        name="moe_combine",
    )(dest, y_sorted, gates, h, lng, lnb)


def _moe_plan(eidx, rank, counts, n_tiles, expert0):
    tm = MOE_TILE
    cnt = counts[0, :N_EXPERTS]
    padded = (cnt + tm - 1) // tm * tm
    ends = jnp.cumsum(padded)
    starts = ends - padded
    dest = (starts[eidx] + rank).reshape(-1).astype(jnp.int32)
    tile_start = jnp.arange(n_tiles, dtype=jnp.int32) * tm
    tile_expert = jnp.sum((tile_start[:, None] >= ends[None, :]).astype(jnp.int32), axis=1)
    tile_expert = (jnp.minimum(tile_expert, N_EXPERTS - 1) + expert0).astype(jnp.int32)
    n_used = (ends[-1] // tm).astype(jnp.int32).reshape(1)
    return dest, tile_expert, n_used


def _layer(x2d, lw, shared, layer, depth, dims, alpha, kv_prev):
    nb_p, seq_p, nb_s, seq_s = dims
    t_p = nb_p * seq_p
    t_s = nb_s * seq_s
    page_table = shared["page_table"]
    page0 = layer * shared["n_pool"]
    proj = _project(x2d, lw["wt_main"])
    kbuf, vbuf = _kv_prompt(x2d, lw["wt_kv"], layer, depth, nb_p, seq_p, kv_prev)
    kv_s = _project(x2d, lw["wt_kv"], row0=t_p, rows=t_s)
    logft_p, cum_p, cumt_p = _fox_gate_prompt(x2d, lw["wt_f"], lw["b_f"], nb_p, seq_p)
    oa_p, s_p = _hgrn(proj, lw["lb"], lw["hgrn_g"], None, 0, 0, nb_p, seq_p)
    oa_s, s_s = _hgrn(proj, lw["lb"], lw["hgrn_g"], shared["state"], layer * nb_s, t_p, nb_s, seq_s)
    ob_p = _fox_prompt(proj, kbuf, vbuf, layer, cum_p, cumt_p, nb_p, seq_p)
    ob_s, logft_s = _fox_sample(proj, kv_s, x2d, lw["wt_f"], lw["b_f"], shared["cache_kt"], shared["cache_vt"],
                                shared["cache_logf_t"], page_table, page0, t_p, nb_s, seq_s)
    h, eidx, gates, rank, counts = _mix(oa_p, oa_s, ob_p, ob_s, proj, x2d, lw["w_a"], lw["w_b"], lw["w_o"],
                                        lw["ln_mix_g"], lw["ln_mix_b"], lw["wr_hi"], lw["wr_lo"], lw["b_r"], alpha)
    t = x2d.shape[0]
    n_tiles = (t * TOP_K + N_EXPERTS * (MOE_TILE - 1)) // MOE_TILE + 1
    dest, tile_expert, n_used = _moe_plan(eidx, rank, counts, n_tiles, layer * N_EXPERTS)
    x_sorted = _dispatch(h, dest, n_tiles * MOE_TILE)
    y_sorted = _experts(x_sorted, tile_expert, n_used,
                        shared["w_gate_up"], shared["b_gate_up"], shared["w_down"], shared["b_down"])
    y = _combine(y_sorted, dest, gates, h, lw["ln_ffn_g"], lw["ln_ffn_b"], alpha)
    return y, (kbuf, vbuf), kv_s, logft_p, logft_s, s_p, s_s


def kernel(x_prompt, x_sample, cache_k, cache_v, cache_logf, state_hgrn, page_table, w_in, b_fox_f, lb_logits,
           hgrn_norm_g, w_branch_a, w_branch_b, w_out, ln_mix_g, ln_mix_b, w_router, b_router, w_gate_up,
           b_gate_up, w_down, b_down, ln_ffn_g, ln_ffn_b):
    depth = w_in.shape[0]
    nb_p, seq_p, d = x_prompt.shape
    nb_s, seq_s, _ = x_sample.shape
    t_p, t_s = nb_p * seq_p, nb_s * seq_s
    alpha = (2 * depth) ** 0.25
    n_pool, page = cache_k.shape[1], cache_k.shape[2]
    dff2 = w_gate_up.shape[-1]

    pl_soft = jax.nn.softmax(lb_logits.astype(F32), axis=0)
    lower_bounds = jnp.cumsum(pl_soft, axis=0) - pl_soft[0:1]

    x2d = jnp.concatenate([x_prompt.reshape(t_p, d), x_sample.reshape(t_s, d)], axis=0)
    shared = {
        "page_table": page_table, "n_pool": n_pool,
        "cache_kt": cache_k.transpose(0, 1, 3, 4, 2).reshape(depth * n_pool, D_MODEL, page),
        "cache_vt": cache_v.transpose(0, 1, 3, 4, 2).reshape(depth * n_pool, D_MODEL, page),
        "cache_logf_t": cache_logf.transpose(0, 1, 3, 2).reshape(depth * n_pool, B_HEADS, page),
        "state": state_hgrn.reshape(depth * nb_s, A_HEADS, A_DK, A_DK),
        "w_gate_up": w_gate_up.reshape(depth * N_EXPERTS, d, dff2),
        "b_gate_up": b_gate_up.reshape(depth * N_EXPERTS, 1, dff2),
        "w_down": w_down.reshape(depth * N_EXPERTS, dff2 // 2, d),
        "b_down": b_down.reshape(depth * N_EXPERTS, 1, d),
    }
    wt_in = w_in.transpose(0, 2, 1)
    c_k = 5 * D_MODEL
    c_f = 7 * D_MODEL
    pad_r = LANES - N_EXPERTS
    kv_prev = None
    fp, sp, ks, vs, fs, ss = [], [], [], [], [], []
    for l in range(depth):
        wt = wt_in[l]
        wr_t = jnp.pad(w_router[l].T, ((0, pad_r), (0, 0)))
        wr_hi = wr_t.astype(BF16)
        lw = {
            "wt_main": jnp.concatenate([wt[:c_k], wt[c_f + B_HEADS:]], axis=0).astype(BF16),
            "wt_kv": wt[c_k:c_f].astype(BF16),
            "wt_f": wt[c_f:c_f + B_HEADS].astype(BF16),
            "b_f": b_fox_f[l],
            "lb": lower_bounds[l].reshape(1, d),
            "hgrn_g": hgrn_norm_g[l].reshape(1, A_DK),
            "w_a": w_branch_a[l].astype(BF16), "w_b": w_branch_b[l].astype(BF16), "w_o": w_out[l].astype(BF16),
            "ln_mix_g": ln_mix_g[l].reshape(1, d), "ln_mix_b": ln_mix_b[l].reshape(1, d),
            "ln_ffn_g": ln_ffn_g[l].reshape(1, d), "ln_ffn_b": ln_ffn_b[l].reshape(1, d),
            "b_r": jnp.pad(b_router[l], (0, pad_r), constant_values=MASK_VALUE).reshape(1, LANES),
            "wr_hi": wr_hi, "wr_lo": (wr_t - wr_hi.astype(F32)).astype(BF16),
        }
        x2d, kv_prev, kv_s, logft_p, logft_s, s_p, s_s = _layer(
            x2d, lw, shared, l, depth, (nb_p, seq_p, nb_s, seq_s), alpha, kv_prev)
        ks.append(kv_s[:, :D_MODEL].reshape(nb_s, seq_s, B_HEADS, B_DH))
        vs.append(kv_s[:, D_MODEL:].reshape(nb_s, seq_s, B_HEADS, B_DH))
        fp.append(logft_p)
        fs.append(logft_s)
        sp.append(s_p)
        ss.append(s_s)
    y_p = x2d[:t_p].reshape(nb_p, seq_p, d)
    y_s = x2d[t_p:].reshape(nb_s, seq_s, d)
    kbuf, vbuf = kv_prev
    k_p = kbuf.reshape(depth, nb_p, B_HEADS, B_DH, seq_p).transpose(0, 1, 4, 2, 3)
    v_p = vbuf.reshape(depth, nb_p, B_HEADS, B_DH, seq_p).transpose(0, 1, 4, 2, 3)
    f_p = jnp.stack(fp).transpose(0, 1, 3, 2)
    f_s = jnp.stack(fs).transpose(0, 1, 3, 2)
    return (y_p, y_s, k_p, v_p, f_p, jnp.stack(sp), jnp.stack(ks), jnp.stack(vs), f_s, jnp.stack(ss))
```

```python
import functools

import jax
import jax.numpy as jnp
from jax import lax
from jax.experimental import pallas as pl
from jax.experimental.pallas import tpu as pltpu

F32 = jnp.float32
BF16 = jnp.bfloat16

D_MODEL = 1024
A_HEADS = 8
A_DK = 128
B_HEADS = 16
B_DH = 64
N_EXPERTS = 32
TOP_K = 4
SWIGLU_LIMIT = 7.0
SWIGLU_ALPHA = 1.702
LN_EPS = 1e-5
RMS_EPS = 1e-6
MASK_VALUE = -1e30
LOG2E = 1.4426950408889634
LANES = 128
HGRN_CHUNK = 64
HGRN_SAFE_DECAY = 80.0
VMEM_LIMIT = 56 * 1024 * 1024
PROJ_Q0 = 4 * D_MODEL
PROJ_G0 = 5 * D_MODEL
MOE_TILE = 256


def _pick_tile(n, candidates):
    for c in candidates:
        if n % c == 0:
            return c
    return n


def _cparams(sem, vmem=None):
    return pltpu.CompilerParams(dimension_semantics=sem, vmem_limit_bytes=vmem or VMEM_LIMIT)


def _split3(x):
    hi = x.astype(BF16)
    r = x - hi.astype(F32)
    mid = r.astype(BF16)
    lo = (r - mid.astype(F32)).astype(BF16)
    return hi, mid, lo


def _dot(a, b):
    return jnp.dot(a, b, preferred_element_type=F32)


def _dot_nt(a, b):
    return lax.dot_general(a, b, (((1,), (1,)), ((), ())), preferred_element_type=F32)


def _dot_tn(a, b):
    return lax.dot_general(a, b, (((0,), (0,)), ((), ())), preferred_element_type=F32)


def _tri(n, upper=False):
    r = lax.broadcasted_iota(jnp.int32, (n, n), 0)
    c = lax.broadcasted_iota(jnp.int32, (n, n), 1)
    return ((r <= c) if upper else (c <= r)).astype(BF16)


def _cumsum_rows(x):
    tri = _tri(x.shape[0])
    hi, mid, lo = _split3(x)
    return _dot(tri, hi) + _dot(tri, mid) + _dot(tri, lo)


def _cumsum_lanes(x):
    tri = _tri(x.shape[1], upper=True)
    hi, mid, lo = _split3(x)
    return _dot(hi, tri) + _dot(mid, tri) + _dot(lo, tri)


def _log_sigmoid(z):
    return jnp.minimum(z, 0.0) - jnp.log(1.0 + jnp.exp(-jnp.abs(z)))


def _sigmoid(z):
    return 1.0 / (1.0 + jnp.exp(-z))


def _silu(z):
    return z * _sigmoid(z)


def _layer_norm(x, g, b):
    mu = jnp.mean(x, axis=-1, keepdims=True)
    xc = x - mu
    var = jnp.mean(xc * xc, axis=-1, keepdims=True)
    return xc * lax.rsqrt(var + LN_EPS) * g + b


def _drop_ref(kern, idx):
    def wrapped(*refs):
        return kern(*(refs[:idx] + refs[idx + 1:]))
    return wrapped


def _proj_kernel(x_ref, wt_ref, o_ref):
    o_ref[...] = _dot_nt(x_ref[...].astype(BF16), wt_ref[...])


def _project(x2d, wt, row0=0, rows=None):
    d = x2d.shape[1]
    t = x2d.shape[0] if rows is None else rows
    n = wt.shape[0]
    tm = _pick_tile(t, (1280, 640, 512, 256))
    tn = _pick_tile(n, (1024, 512, 256, 128))
    rb0 = row0 // tm
    return pl.pallas_call(
        _proj_kernel,
        grid=(t // tm, n // tn),
        in_specs=[pl.BlockSpec((tm, d), lambda i, j: (rb0 + i, 0)),
                  pl.BlockSpec((tn, d), lambda i, j: (j, 0))],
        out_specs=pl.BlockSpec((tm, tn), lambda i, j: (i, j)),
        out_shape=jax.ShapeDtypeStruct((t, n), F32),
        compiler_params=_cparams(("parallel", "parallel")),
        name="in_proj",
    )(x2d, wt)


def _kv_prompt_kernel(x_ref, wt_ref, k_ref, v_ref, *, layer, fill_others):
    def compute():
        kv = _dot_nt(wt_ref[...], x_ref[...].astype(BF16))
        k_ref[0, 0] = kv[:D_MODEL]
        v_ref[0, 0] = kv[D_MODEL:]

    if not fill_others:
        compute()
        return
    s = pl.program_id(0)
    pl.when(s == layer)(compute)

    @pl.when(s != layer)
    def _():
        k_ref[...] = jnp.zeros_like(k_ref)
        v_ref[...] = jnp.zeros_like(v_ref)


def _kv_prompt(x2d, wt_kv, layer, depth, nb, seq, prev):
    d = x2d.shape[1]
    tl = _pick_tile(seq, (512, 256, 128))
    nl = seq // tl
    first = prev is None
    in_specs = [pl.BlockSpec((tl, d), lambda s, b, i: (b * nl + i, 0)),
                pl.BlockSpec((2 * D_MODEL, d), lambda s, b, i: (0, 0))]
    args = [x2d, wt_kv]
    kern = functools.partial(_kv_prompt_kernel, layer=layer, fill_others=first)
    aliases = {}
    if first:
        ospec = pl.BlockSpec((1, 1, D_MODEL, tl), lambda s, b, i: (s, b, 0, i))
    else:
        in_specs += [pl.BlockSpec(memory_space=pl.ANY)] * 2
        args += list(prev)
        aliases = {2: 0, 3: 1}
        kern = _drop_ref(_drop_ref(kern, 2), 3)
        ospec = pl.BlockSpec((1, 1, D_MODEL, tl), lambda s, b, i: (layer, b, 0, i))
    out = jax.ShapeDtypeStruct((depth, nb, D_MODEL, seq), F32)
    return pl.pallas_call(
        kern,
        grid=(depth if first else 1, nb, nl),
        in_specs=in_specs,
        out_specs=[ospec, ospec],
        out_shape=[out, out],
        input_output_aliases=aliases,
        compiler_params=_cparams(("parallel", "parallel", "parallel")),
        name="kv_prompt",
    )(*args)


def _fox_gate_prompt_kernel(x_ref, wft_ref, bf_ref, bft_ref, logft_ref, cum_ref, cumt_ref, carry_ref, carryt_ref):
    i = pl.program_id(1)

    @pl.when(i == 0)
    def _():
        carry_ref[...] = jnp.zeros_like(carry_ref)
        carryt_ref[...] = jnp.zeros_like(carryt_ref)

    xb = x_ref[...].astype(BF16)
    logf = _log_sigmoid(_dot_nt(xb, wft_ref[...]) + bf_ref[...])
    logft = _log_sigmoid(_dot_nt(wft_ref[...], xb) + bft_ref[...])
    tl = logf.shape[0]
    cs = _cumsum_rows(logf) + carry_ref[...]
    cst = _cumsum_lanes(logft) + carryt_ref[...]
    carry_ref[...] = cs[tl - 1:tl, :]
    carryt_ref[...] = cst[:, tl - 1:tl]
    logft_ref[0] = logft
    cum_ref[0] = cs
    cumt_ref[0] = cst


def _fox_gate_prompt(x2d, wft, bf, nb, seq):
    d = x2d.shape[1]
    tl = _pick_tile(seq, (256, 128))
    nl = seq // tl
    out_t = jax.ShapeDtypeStruct((nb, B_HEADS, seq), F32)
    tspec = pl.BlockSpec((1, B_HEADS, tl), lambda b, i: (b, 0, i))
    return pl.pallas_call(
        _fox_gate_prompt_kernel,
        grid=(nb, nl),
        in_specs=[pl.BlockSpec((tl, d), lambda b, i: (b * nl + i, 0)),
                  pl.BlockSpec((B_HEADS, d), lambda b, i: (0, 0)),
                  pl.BlockSpec((1, B_HEADS), lambda b, i: (0, 0)),
                  pl.BlockSpec((B_HEADS, 1), lambda b, i: (0, 0))],
        out_specs=[tspec, pl.BlockSpec((1, tl, B_HEADS), lambda b, i: (b, i, 0)), tspec],
        out_shape=[out_t, jax.ShapeDtypeStruct((nb, seq, B_HEADS), F32), out_t],
        scratch_shapes=[pltpu.VMEM((1, B_HEADS), F32), pltpu.VMEM((B_HEADS, 1), F32)],
        compiler_params=_cparams(("parallel", "arbitrary")),
        name="fox_gate_prompt",
    )(x2d, wft, bf.reshape(1, B_HEADS), bf.reshape(B_HEADS, 1))


def _hgrn_kernel(*refs, chunk, n_chunks, has_s0):
    if has_s0:
        (aq_ref, af_ref, ai_ref, ag_ref, lb_ref, g_ref, s0_ref, o_ref, snew_ref,
         st_ref, a_ref, gs_ref, ks_ref, qt_ref, kt_ref, vb_ref) = refs
    else:
        (aq_ref, af_ref, ai_ref, ag_ref, lb_ref, g_ref, o_ref, snew_ref,
         st_ref, a_ref, gs_ref, ks_ref, qt_ref, kt_ref, vb_ref) = refs
        s0_ref = None
    i = pl.program_id(1)
    c = chunk
    rows = c * n_chunks

    @pl.when(i == 0)
    def _():
        for h in range(A_HEADS):
            if has_s0:
                st_ref[h] = s0_ref[0, h].T
            else:
                st_ref[h] = jnp.zeros((A_DK, A_DK), F32)

    lbv = lb_ref[...]
    gv = g_ref[...]
    scale = A_DK ** -0.5

    z = af_ref[...]
    logf = jnp.log(lbv + (1.0 - lbv) * _sigmoid(z))
    kk = (1.0 - lbv) * _sigmoid(-z)
    rr = lax.broadcasted_iota(jnp.int32, (rows, rows), 0)
    cc = lax.broadcasted_iota(jnp.int32, (rows, rows), 1)
    tri = ((rr // c == cc // c) & (cc <= rr)).astype(BF16)
    hi, mid, lo = _split3(logf)
    g_cum = _dot(tri, hi) + _dot(tri, mid) + _dot(tri, lo)
    gs_ref[...] = g_cum
    ks_ref[...] = kk
    qt_ref[...] = (_silu(aq_ref[...]) * scale * jnp.exp(g_cum)).astype(BF16)
    kt_ref[...] = (kk * jnp.exp(-g_cum)).astype(BF16)
    vb_ref[...] = ai_ref[...].astype(BF16)
    a_ref[...] = jnp.zeros_like(a_ref)

    row = lax.broadcasted_iota(jnp.int32, (c, c), 0)
    col = lax.broadcasted_iota(jnp.int32, (c, c), 1)
    causal = col <= row

    safes = []
    for ci in range(n_chunks):
        r = slice(ci * c, (ci + 1) * c)
        safe = jnp.min(gs_ref[(ci + 1) * c - 1:(ci + 1) * c, :]) >= -HGRN_SAFE_DECAY
        safes.append(safe)

        @pl.when(jnp.logical_not(safe))
        def _():
            q = _silu(aq_ref[r, :]) * scale
            g_c = gs_ref[r, :]

            def col_j(j, carry):
                gj = gs_ref[pl.ds(ci * c + j, 1), :]
                kj = ks_ref[pl.ds(ci * c + j, 1), :]
                w = q * kj * jnp.exp(jnp.minimum(g_c - gj, 0.0))
                for h in range(A_HEADS):
                    hs = slice(h * A_DK, (h + 1) * A_DK)
                    cj = jnp.sum(w[:, hs], axis=1, keepdims=True)
                    a_ref[ci, h] = a_ref[ci, h] + jnp.where(col == j, cj, 0.0)
                return carry

            lax.fori_loop(0, c, col_j, 0)

    for ci in range(n_chunks):
        r = slice(ci * c, (ci + 1) * c)
        safe = safes[ci]
        g_c = gs_ref[r, :]
        gend = g_c[c - 1:c, :]
        khat = (ks_ref[r, :] * jnp.exp(gend - g_c)).astype(BF16)
        egend = jnp.exp(gend)
        for h in range(A_HEADS):
            hs = slice(h * A_DK, (h + 1) * A_DK)
            qt = qt_ref[r, hs]
            vb = vb_ref[r, hs]
            a_fast = _dot_nt(qt, kt_ref[r, hs])
            att = jnp.where(causal, jnp.where(safe, a_fast, a_ref[ci, h]), 0.0).astype(BF16)
            st = st_ref[h]
            o = _dot(att, vb) + _dot_nt(qt, st.astype(BF16))
            st_ref[h] = st * egend[:, hs] + _dot_tn(vb, khat[:, hs])
            ms = jnp.mean(o * o, axis=-1, keepdims=True)
            o = o * lax.rsqrt(ms + RMS_EPS) * gv
            o_ref[r, hs] = o * _silu(ag_ref[r, hs])

    @pl.when(i == pl.num_programs(1) - 1)
    def _():
        for h in range(A_HEADS):
            snew_ref[0, h] = st_ref[h].T


def _hgrn(proj, lb, g, s0, s0_base, row0, nb, seq):
    chunk = min(HGRN_CHUNK, seq)
    lb_rows = _pick_tile(seq, (256, chunk))
    nl = seq // lb_rows
    rb0 = row0 // lb_rows
    has_s0 = s0 is not None
    in_specs = [pl.BlockSpec((lb_rows, D_MODEL), functools.partial(lambda b, i, k: (rb0 + b * nl + i, k), k=k))
                for k in range(4)]
    in_specs += [pl.BlockSpec((1, D_MODEL), lambda b, i: (0, 0)),
                 pl.BlockSpec((1, A_DK), lambda b, i: (0, 0))]
    args = [proj, proj, proj, proj, lb, g]
    if has_s0:
        in_specs.append(pl.BlockSpec((1, A_HEADS, A_DK, A_DK), lambda b, i: (s0_base + b, 0, 0, 0)))
        args.append(s0)
    kern = functools.partial(_hgrn_kernel, chunk=chunk, n_chunks=lb_rows // chunk, has_s0=has_s0)
    return pl.pallas_call(
        kern,
        grid=(nb, nl),
        in_specs=in_specs,
        out_specs=[pl.BlockSpec((lb_rows, D_MODEL), lambda b, i: (b * nl + i, 0)),
                   pl.BlockSpec((1, A_HEADS, A_DK, A_DK), lambda b, i: (b, 0, 0, 0))],
        out_shape=[jax.ShapeDtypeStruct((nb * seq, D_MODEL), F32),
                   jax.ShapeDtypeStruct((nb, A_HEADS, A_DK, A_DK), F32)],
        scratch_shapes=[pltpu.VMEM((A_HEADS, A_DK, A_DK), F32),
                        pltpu.VMEM((lb_rows // chunk, A_HEADS, chunk, chunk), F32),
                        pltpu.VMEM((lb_rows, D_MODEL), F32), pltpu.VMEM((lb_rows, D_MODEL), F32),
                        pltpu.VMEM((lb_rows, D_MODEL), BF16), pltpu.VMEM((lb_rows, D_MODEL), BF16),
                        pltpu.VMEM((lb_rows, D_MODEL), BF16)],
        compiler_params=_cparams(("parallel", "arbitrary")),
        name="hgrn",
    )(*args)


def _fox_prompt_kernel(q_ref, k_ref, v_ref, cum_ref, cumt_ref, o_ref, kn_ref, qt_ref, vt_ref, s_ref, p_ref,
                       *, tq, tk):
    j = pl.program_id(1)
    seq = q_ref.shape[0]
    nq = seq // tq
    hpair = (2 * j, 2 * j + 1)

    kn_ref[:, :LANES] = k_ref[0, 0].T.astype(BF16)
    h16 = lax.broadcasted_iota(jnp.int32, (B_HEADS, LANES), 0)
    l16 = lax.broadcasted_iota(jnp.int32, (B_HEADS, LANES), 1)
    kaug = jnp.zeros((seq, LANES), F32)
    for part, term in enumerate(_split3(cum_ref[0] * LOG2E)):
        sel = jnp.where(((h16 == hpair[0]) & (l16 == part)) | ((h16 == hpair[1]) & (l16 == 3 + part)), -1.0, 0.0)
        kaug = kaug + _dot(term, sel.astype(BF16))
    lane_s = lax.broadcasted_iota(jnp.int32, (seq, LANES), 1)
    kn_ref[:, LANES:] = (kaug + jnp.where((lane_s >= 6) & (lane_s < 9), 1.0, 0.0)).astype(BF16)

    qt = (q_ref[...] * (B_DH ** -0.5 * LOG2E)).T
    row = lax.broadcasted_iota(jnp.int32, (LANES, seq), 0)
    r16 = lax.broadcasted_iota(jnp.int32, (LANES, B_HEADS), 0)
    c16 = lax.broadcasted_iota(jnp.int32, (LANES, B_HEADS), 1)
    cumt_terms = _split3(cumt_ref[0] * LOG2E)
    for h in range(2):
        qt_ref[h, :LANES, :] = jnp.where(row // B_DH == h, qt, 0.0).astype(BF16)
        qaug = jnp.where((row >= 3 * h) & (row < 3 * h + 3), 1.0, 0.0)
        for part, term in enumerate(cumt_terms):
            sel = jnp.where((r16 == 6 + part) & (c16 == hpair[h]), 1.0, 0.0)
            qaug = qaug + _dot(sel.astype(BF16), term)
        qt_ref[h, LANES:, :] = qaug.astype(BF16)
    vt_ref[...] = v_ref[0, 0].astype(BF16)

    kpos = lax.broadcasted_iota(jnp.int32, (tk, tq), 0)
    qpos = lax.broadcasted_iota(jnp.int32, (tk, tq), 1)
    kpq = tq // tk

    sub = tk // 8

    def fold(x):
        return x.reshape(sub, 8, tq)

    for qi in range(nq):
        q0 = qi * tq
        nfull = qi * kpq
        outs = []
        for h in range(2):
            qa = qt_ref[h, :, q0:q0 + tq]

            def scores(ki, mx, qa=qa, h=h):
                ks = pl.ds(pl.multiple_of(ki * tk, tk), tk)
                s = _dot(kn_ref[ks, :], qa)
                s_ref[h, ks, :] = s
                return jnp.maximum(mx, jnp.max(fold(s), axis=0))

            mx = jnp.full((8, tq), -jnp.inf, F32)
            if nfull:
                mx = lax.fori_loop(0, nfull, scores, mx, unroll=2)
            for dk in range(kpq):
                k0 = (nfull + dk) * tk
                s = _dot(kn_ref[k0:k0 + tk, :], qa)
                s = jnp.where(k0 + kpos <= q0 + qpos, s, MASK_VALUE)
                s_ref[h, k0:k0 + tk, :] = s
                mx = jnp.maximum(mx, jnp.max(fold(s), axis=0))
            m = jnp.max(mx, axis=0, keepdims=True)

            def probs(ki, lsum, m=m, h=h):
                ks = pl.ds(pl.multiple_of(ki * tk, tk), tk)
                p = jnp.exp2(s_ref[h, ks, :] - m)
                p_ref[h, ks, :] = p.astype(BF16)
                return lsum + jnp.sum(fold(p), axis=0)

            nk = nfull + kpq
            lsum = lax.fori_loop(0, nk, probs, jnp.zeros((8, tq), F32), unroll=2)
            l = jnp.sum(lsum, axis=0, keepdims=True)
            acc = _dot(vt_ref[h * B_DH:(h + 1) * B_DH, 0:nk * tk], p_ref[h, 0:nk * tk, :])
            outs.append(acc * (1.0 / l))
        o_ref[q0:q0 + tq, :] = jnp.concatenate(outs, axis=0).T


def _fox_prompt(proj, kbuf, vbuf, layer, cum, cum_t, nb, seq):
    t = nb * seq
    tq = _pick_tile(seq, (512, 256, 128))
    tk = min(tq, 256)
    npair = B_HEADS // 2
    c0 = PROJ_Q0 // LANES
    kvspec = pl.BlockSpec((1, 1, LANES, seq), lambda b, j: (layer, b, j, 0))
    return pl.pallas_call(
        functools.partial(_fox_prompt_kernel, tq=tq, tk=tk),
        grid=(nb, npair),
        in_specs=[pl.BlockSpec((seq, LANES), lambda b, j: (b, c0 + j)), kvspec, kvspec,
                  pl.BlockSpec((1, seq, B_HEADS), lambda b, j: (b, 0, 0)),
                  pl.BlockSpec((1, B_HEADS, seq), lambda b, j: (b, 0, 0))],
        out_specs=pl.BlockSpec((seq, LANES), lambda b, j: (b, j)),
        out_shape=jax.ShapeDtypeStruct((t, D_MODEL), F32),
        scratch_shapes=[pltpu.VMEM((seq, 2 * LANES), BF16), pltpu.VMEM((2, 2 * LANES, seq), BF16),
                        pltpu.VMEM((LANES, seq), BF16), pltpu.VMEM((2, seq, tq), F32),
                        pltpu.VMEM((2, seq, tq), BF16)],
        compiler_params=_cparams(("parallel", "parallel")),
        name="fox_prompt",
    )(proj, kbuf, vbuf, cum, cum_t)


def _fox_sample_kernel(*refs, pages_per_step, ns):
    pps = pages_per_step
    q_ref, kvn_ref, x_ref, wft_ref, bft_ref = refs[1:6]
    k_refs = refs[6:6 + pps]
    v_refs = refs[6 + pps:6 + 2 * pps]
    f_refs = refs[6 + 2 * pps:6 + 3 * pps]
    o_ref, logft_ref = refs[6 + 3 * pps:8 + 3 * pps]
    qbd_ref, fq_ref, cn_ref, past_ref, m_ref, l_ref, acc_ref = refs[8 + 3 * pps:]
    g = pl.program_id(1)
    nrow = B_HEADS * ns
    rh = lax.broadcasted_iota(jnp.int32, (nrow, D_MODEL), 0) // ns
    ch = lax.broadcasted_iota(jnp.int32, (nrow, D_MODEL), 1) // B_DH

    def expand_rows(x):
        n = x.shape[1]
        return jnp.broadcast_to(x[:, None, :], (B_HEADS, ns, n)).reshape(nrow, n)

    @pl.when(g == 0)
    def _():
        q = q_ref[...] * (B_DH ** -0.5)
        qrep = jnp.broadcast_to(q[None], (B_HEADS, ns, D_MODEL)).reshape(nrow, D_MODEL)
        qbd_ref[...] = jnp.where(rh == ch, qrep, 0.0).astype(BF16)
        logft = _log_sigmoid(_dot_nt(wft_ref[...], x_ref[...].astype(BF16)) + bft_ref[...])
        logft_ref[0] = logft
        cn = _cumsum_lanes(logft)
        cn_ref[...] = cn
        tsel = (lax.broadcasted_iota(jnp.int32, (nrow, ns), 0) % ns
                == lax.broadcasted_iota(jnp.int32, (nrow, ns), 1))
        fq_ref[...] = jnp.sum(jnp.where(tsel, expand_rows(cn), 0.0), axis=1, keepdims=True)
        past_ref[...] = jnp.zeros_like(past_ref)
        m_ref[...] = jnp.full_like(m_ref, -jnp.inf)
        l_ref[...] = jnp.zeros_like(l_ref)
        acc_ref[...] = jnp.zeros_like(acc_ref)

    def update(s, pv):
        m_old = m_ref[...]
        m_new = jnp.maximum(m_old, jnp.max(s, axis=1, keepdims=True))
        alpha = jnp.exp(m_old - m_new)
        p = jnp.exp(s - m_new)
        l_ref[...] = alpha * l_ref[...] + jnp.sum(p, axis=1, keepdims=True)
        acc_ref[...] = alpha * acc_ref[...] + pv(p.astype(BF16))
        m_ref[...] = m_new

    page = k_refs[0].shape[2]
    lf = jnp.concatenate([f_refs[pi][0] for pi in range(pps)], axis=0)
    cs = _cumsum_lanes(lf)
    offset = past_ref[...]
    s_parts = []
    for pi in range(pps):
        cs_p = cs[pi * B_HEADS:(pi + 1) * B_HEADS, :] + offset
        offset = cs_p[:, page - 1:]
        kt = k_refs[pi][0].astype(BF16)
        s_parts.append(_dot(qbd_ref[...], kt) + fq_ref[...] - expand_rows(cs_p))
    past_ref[...] = offset

    def pv_pages(p):
        out = None
        for pi in range(pps):
            term = _dot_nt(p[:, pi * page:(pi + 1) * page], v_refs[pi][0].astype(BF16))
            out = term if out is None else out + term
        return out

    update(jnp.concatenate(s_parts, axis=1), pv_pages)

    @pl.when(g == pl.num_programs(1) - 1)
    def _():
        kb = kvn_ref[:, :D_MODEL].astype(BF16)
        vb = kvn_ref[:, D_MODEL:].astype(BF16)
        fk = expand_rows(cn_ref[...] + past_ref[...])
        s = _dot_nt(qbd_ref[...], kb) + fq_ref[...] - fk
        tq = lax.broadcasted_iota(jnp.int32, (nrow, ns), 0) % ns
        tk = lax.broadcasted_iota(jnp.int32, (nrow, ns), 1)
        s = jnp.where(tq >= tk, s, MASK_VALUE)
        update(s, lambda p: _dot(p, vb))
        o = acc_ref[...] / l_ref[...]
        for h in range(B_HEADS):
            o_ref[:, h * B_DH:(h + 1) * B_DH] = o[h * ns:(h + 1) * ns, h * B_DH:(h + 1) * B_DH]


def _fox_sample(proj, kv_new, x2d, wft, bf, cache_kt, cache_vt, cache_logf_t, page_table, page0, row0, nb, ns):
    d = x2d.shape[1]
    n_pages = page_table.shape[1]
    page = cache_kt.shape[2]
    pps = _pick_tile(n_pages, (8, 4, 2, 1))
    ng = n_pages // pps
    rb0 = row0 // ns
    nrow = B_HEADS * ns

    def page_spec(pi, rows):
        return pl.BlockSpec((1, rows, page), lambda b, g, pt: (page0 + pt[b, g * pps + pi], 0, 0))

    in_specs = [pl.BlockSpec((ns, D_MODEL), lambda b, g, pt: (rb0 + b, PROJ_Q0 // D_MODEL)),
                pl.BlockSpec((ns, 2 * D_MODEL), lambda b, g, pt: (b, 0)),
                pl.BlockSpec((ns, d), lambda b, g, pt: (rb0 + b, 0)),
                pl.BlockSpec((B_HEADS, d), lambda b, g, pt: (0, 0)),
                pl.BlockSpec((B_HEADS, 1), lambda b, g, pt: (0, 0))]
    in_specs += [page_spec(pi, D_MODEL) for pi in range(pps)] * 2
    in_specs += [page_spec(pi, B_HEADS) for pi in range(pps)]
    args = [proj, kv_new, x2d, wft, bf.reshape(B_HEADS, 1)]
    args += [cache_kt] * pps + [cache_vt] * pps + [cache_logf_t] * pps
    grid_spec = pltpu.PrefetchScalarGridSpec(
        num_scalar_prefetch=1,
        grid=(nb, ng),
        in_specs=in_specs,
        out_specs=[pl.BlockSpec((ns, D_MODEL), lambda b, g, pt: (b, 0)),
                   pl.BlockSpec((1, B_HEADS, ns), lambda b, g, pt: (b, 0, 0))],
        scratch_shapes=[pltpu.VMEM((nrow, D_MODEL), BF16), pltpu.VMEM((nrow, 1), F32),
                        pltpu.VMEM((B_HEADS, ns), F32), pltpu.VMEM((B_HEADS, 1), F32),
                        pltpu.VMEM((nrow, 1), F32), pltpu.VMEM((nrow, 1), F32),
                        pltpu.VMEM((nrow, D_MODEL), F32)],
    )
    return pl.pallas_call(
        functools.partial(_fox_sample_kernel, pages_per_step=pps, ns=ns),
        grid_spec=grid_spec,
        out_shape=[jax.ShapeDtypeStruct((nb * ns, D_MODEL), F32),
                   jax.ShapeDtypeStruct((nb, B_HEADS, ns), F32)],
        compiler_params=_cparams(("parallel", "arbitrary")),
        name="fox_sample",
    )(page_table, *args)


def _mix_kernel(oap_ref, oas_ref, obp_ref, obs_ref, ga_ref, gb_ref, x_ref, wa_ref, wb_ref, wo_ref, lng_ref,
                lnb_ref, wrh_ref, wrl_ref, br_ref, h_ref, eidx_ref, gate_ref, rank_ref, cnt_ref, carry_ref,
                *, alpha, n_prompt_tiles):
    i = pl.program_id(0)

    @pl.when(i == 0)
    def _():
        carry_ref[...] = jnp.zeros_like(carry_ref)

    is_prompt = i < n_prompt_tiles
    oa = jnp.where(is_prompt, oap_ref[...], oas_ref[...])
    ob = jnp.where(is_prompt, obp_ref[...], obs_ref[...])
    ya = _dot(oa.astype(BF16), wa_ref[...])
    yb = _dot(ob.astype(BF16), wb_ref[...])
    m = _sigmoid(ga_ref[...]) * ya + _sigmoid(gb_ref[...]) * yb
    mix = _dot(m.astype(BF16), wo_ref[...])
    h = _layer_norm(alpha * x_ref[...] + mix, lng_ref[...], lnb_ref[...])
    h_ref[...] = h

    h_hi = h.astype(BF16)
    h_lo = (h - h_hi.astype(F32)).astype(BF16)
    logits = (_dot_nt(h_hi, wrh_ref[...]) + _dot_nt(h_lo, wrh_ref[...]) + _dot_nt(h_hi, wrl_ref[...])
              + br_ref[...])
    tm = logits.shape[0]
    lane = lax.broadcasted_iota(jnp.int32, (tm, LANES), 1)
    work = logits
    idxs, vals = [], []
    for _ in range(TOP_K):
        mx = jnp.max(work, axis=1, keepdims=True)
        ix = jnp.min(jnp.where(work == mx, lane, LANES), axis=1, keepdims=True)
        idxs.append(ix)
        vals.append(mx)
        work = jnp.where(lane == ix, -jnp.inf, work)
    exps = [jnp.exp(v - vals[0]) for v in vals]
    denom = exps[0] + exps[1] + exps[2] + exps[3]
    multihot = jnp.zeros((tm, LANES), F32)
    for ix in idxs:
        multihot = multihot + jnp.where(lane == ix, 1.0, 0.0)
    r = lax.broadcasted_iota(jnp.int32, (tm, tm), 0)
    c = lax.broadcasted_iota(jnp.int32, (tm, tm), 1)
    before = _dot((c < r).astype(BF16), multihot.astype(BF16)) + carry_ref[...]
    carry_ref[...] = carry_ref[...] + jnp.sum(multihot, axis=0, keepdims=True)
    col4 = lax.broadcasted_iota(jnp.int32, (tm, TOP_K), 1)
    eidx = jnp.zeros((tm, TOP_K), jnp.int32)
    gates = jnp.zeros((tm, TOP_K), F32)
    ranks = jnp.zeros((tm, TOP_K), F32)
    for k in range(TOP_K):
        rk = jnp.sum(jnp.where(lane == idxs[k], before, 0.0), axis=1, keepdims=True)
        eidx = jnp.where(col4 == k, idxs[k], eidx)
        gates = jnp.where(col4 == k, exps[k] / denom, gates)
        ranks = jnp.where(col4 == k, rk, ranks)
    eidx_ref[...] = eidx
    gate_ref[...] = gates
    rank_ref[...] = ranks.astype(jnp.int32)
    cnt_ref[...] = carry_ref[...].astype(jnp.int32)


def _mix(oa_p, oa_s, ob_p, ob_s, proj, x2d, wa, wb, wo, lng, lnb, wrh, wrl, br, alpha):
    t = x2d.shape[0]
    t_p, t_s = oa_p.shape[0], oa_s.shape[0]
    tm = next(c for c in (256, 128, 64, 32, 16, 8) if t_p % c == 0 and t_s % c == 0)
    npt = t_p // tm
    row = lambda i: (i, 0)
    prow = lambda i: (jnp.minimum(i, npt - 1), 0)
    srow = lambda i: (jnp.maximum(i - npt, 0), 0)
    const = lambda i: (0, 0)
    wspec = pl.BlockSpec((D_MODEL, D_MODEL), const)
    vspec = pl.BlockSpec((1, D_MODEL), const)
    return pl.pallas_call(
        functools.partial(_mix_kernel, alpha=alpha, n_prompt_tiles=npt),
        grid=(t // tm,),
        in_specs=[pl.BlockSpec((tm, D_MODEL), prow), pl.BlockSpec((tm, D_MODEL), srow),
                  pl.BlockSpec((tm, D_MODEL), prow), pl.BlockSpec((tm, D_MODEL), srow),
                  pl.BlockSpec((tm, D_MODEL), lambda i: (i, PROJ_G0 // D_MODEL)),
                  pl.BlockSpec((tm, D_MODEL), lambda i: (i, PROJ_G0 // D_MODEL + 1)),
                  pl.BlockSpec((tm, D_MODEL), row), wspec, wspec, wspec, vspec, vspec,
                  pl.BlockSpec((LANES, D_MODEL), const), pl.BlockSpec((LANES, D_MODEL), const),
                  pl.BlockSpec((1, LANES), const)],
        out_specs=[pl.BlockSpec((tm, D_MODEL), row), pl.BlockSpec((tm, TOP_K), row),
                   pl.BlockSpec((tm, TOP_K), row), pl.BlockSpec((tm, TOP_K), row),
                   pl.BlockSpec((1, LANES), const)],
        out_shape=[jax.ShapeDtypeStruct((t, D_MODEL), F32), jax.ShapeDtypeStruct((t, TOP_K), jnp.int32),
                   jax.ShapeDtypeStruct((t, TOP_K), F32), jax.ShapeDtypeStruct((t, TOP_K), jnp.int32),
                   jax.ShapeDtypeStruct((1, LANES), jnp.int32)],
        scratch_shapes=[pltpu.VMEM((1, LANES), F32)],
        compiler_params=_cparams(("arbitrary",)),
        name="mix_router",
    )(oa_p, oa_s, ob_p, ob_s, proj, proj, x2d, wa, wb, wo, lng, lnb, wrh, wrl, br)


def _dispatch_kernel(dest_ref, last_ref, has_ref, nused_ref, h_ref, x_hbm, stage, zeros, sems, zsem,
                     *, tt, n_tiles, min_used):
    i = pl.program_id(0)
    n = pl.num_programs(0)
    slot = i % 2
    tm = MOE_TILE

    @pl.when(i == 0)
    def _():
        zeros[...] = jnp.zeros_like(zeros)

        def fill(tile):
            return pltpu.make_async_copy(zeros, x_hbm.at[pl.ds(tile * tm, tm), :], zsem)

        for e in range(N_EXPERTS):
            pl.when(has_ref[e] > 0)(lambda e=e: fill(last_ref[e]).start())
        for j in range(n_tiles - min_used):
            pl.when(nused_ref[0] + j < n_tiles)(lambda j=j: fill(nused_ref[0] + j).start())
        for e in range(N_EXPERTS):
            pl.when(has_ref[e] > 0)(lambda e=e: fill(last_ref[e]).wait())
        for j in range(n_tiles - min_used):
            pl.when(nused_ref[0] + j < n_tiles)(lambda j=j: fill(nused_ref[0] + j).wait())

    stage[slot] = h_ref[...]
    for r in range(tt):
        for k in range(TOP_K):
            row = dest_ref[(i * tt + r) * TOP_K + k]
            pltpu.make_async_copy(stage.at[slot, pl.ds(r, 1), :], x_hbm.at[pl.ds(row, 1), :],
                                  sems.at[slot]).start(priority=k % 2)

    def wait_slot(s):
        for _ in range(TOP_K):
            pltpu.make_async_copy(stage.at[s], x_hbm.at[pl.ds(0, tt), :], sems.at[s]).wait()

    @pl.when(i > 0)
    def _():
        wait_slot(1 - slot)

    @pl.when(i == n - 1)
    def _():
        wait_slot(slot)


def _dispatch(h, dest, last_tile, has_rows, n_used, n_tiles):
    t = h.shape[0]
    tt = _pick_tile(t, (256, 128, 64, 16, 8))
    min_used = (t * TOP_K) // MOE_TILE
    grid_spec = pltpu.PrefetchScalarGridSpec(
        num_scalar_prefetch=4,
        grid=(t // tt,),
        in_specs=[pl.BlockSpec((tt, D_MODEL), lambda i, *_: (i, 0))],
        out_specs=pl.BlockSpec(memory_space=pl.ANY),
        scratch_shapes=[pltpu.VMEM((2, tt, D_MODEL), F32), pltpu.VMEM((MOE_TILE, D_MODEL), F32),
                        pltpu.SemaphoreType.DMA((2,)), pltpu.SemaphoreType.DMA],
    )
    return pl.pallas_call(
        functools.partial(_dispatch_kernel, tt=tt, n_tiles=n_tiles, min_used=min_used),
        grid_spec=grid_spec,
        out_shape=jax.ShapeDtypeStruct((n_tiles * MOE_TILE, D_MODEL), F32),
        compiler_params=_cparams(("arbitrary",)),
        name="moe_dispatch",
    )(dest, last_tile, has_rows, n_used, h)


def _expert_kernel(te_ref, nused_ref, x_ref, wgu_ref, bgu_ref, wd_ref, bd_ref, y_ref, wgu_bf, wd_bf):
    i = pl.program_id(0)
    used = i < nused_ref[0]
    prev = te_ref[jnp.maximum(i - 1, 0)]
    fresh = jnp.logical_or(i == 0, te_ref[i] != prev)

    @pl.when(jnp.logical_and(used, fresh))
    def _():
        wgu_bf[...] = wgu_ref[0].astype(BF16)
        wd_bf[...] = wd_ref[0].astype(BF16)

    @pl.when(used)
    def _():
        x = x_ref[...].astype(BF16)
        hcat = _dot(x, wgu_bf[...]) + bgu_ref[0]
        dff = hcat.shape[1] // 2
        gate = jnp.minimum(hcat[:, :dff], SWIGLU_LIMIT)
        up = jnp.clip(hcat[:, dff:], -SWIGLU_LIMIT, SWIGLU_LIMIT)
        act = (up + 1.0) * gate * _sigmoid(SWIGLU_ALPHA * gate)
        y_ref[...] = _dot(act.astype(BF16), wd_bf[...]) + bd_ref[0]

    @pl.when(jnp.logical_not(used))
    def _():
        y_ref[...] = jnp.zeros_like(y_ref)


def _experts(x_sorted, tile_expert, n_used, wgu, bgu, wd, bd):
    tm = MOE_TILE
    n_tiles = x_sorted.shape[0] // tm
    dff2 = wgu.shape[2]
    grid_spec = pltpu.PrefetchScalarGridSpec(
        num_scalar_prefetch=2,
        grid=(n_tiles,),
        in_specs=[pl.BlockSpec((tm, D_MODEL), lambda i, te, n: (jnp.minimum(i, n[0] - 1), 0)),
                  pl.BlockSpec((1, D_MODEL, dff2), lambda i, te, n: (te[i], 0, 0)),
                  pl.BlockSpec((1, 1, dff2), lambda i, te, n: (te[i], 0, 0)),
                  pl.BlockSpec((1, dff2 // 2, D_MODEL), lambda i, te, n: (te[i], 0, 0)),
                  pl.BlockSpec((1, 1, D_MODEL), lambda i, te, n: (te[i], 0, 0))],
        out_specs=pl.BlockSpec((tm, D_MODEL), lambda i, te, n: (i, 0)),
        scratch_shapes=[pltpu.VMEM((D_MODEL, dff2), BF16), pltpu.VMEM((dff2 // 2, D_MODEL), BF16)],
    )
    return pl.pallas_call(
        _expert_kernel,
        grid_spec=grid_spec,
        out_shape=jax.ShapeDtypeStruct((n_tiles * tm, D_MODEL), F32),
        compiler_params=_cparams(("arbitrary",)),
        name="moe_experts",
    )(tile_expert, n_used, x_sorted, wgu, bgu, wd, bd)


def _combine_kernel(dest_ref, y_hbm, gate_ref, h_ref, lng_ref, lnb_ref, o_ref, buf, sems, *, tt, alpha):
    i = pl.program_id(0)
    n = pl.num_programs(0)
    slot = i % 2

    def issue_tile(tile, s):
        for r in range(tt):
            for k in range(TOP_K):
                row = dest_ref[(tile * tt + r) * TOP_K + k]
                pltpu.make_async_copy(y_hbm.at[pl.ds(row, 1), :], buf.at[s, k, pl.ds(r, 1), :],
                                      sems.at[s]).start(priority=k % 2)

    @pl.when(i == 0)
    def _():
        issue_tile(0, 0)

    @pl.when(i + 1 < n)
    def _():
        issue_tile(i + 1, 1 - slot)

    for k in range(TOP_K):
        pltpu.make_async_copy(y_hbm.at[pl.ds(0, tt), :], buf.at[slot, k], sems.at[slot]).wait()
    gates = gate_ref[...]
    ffn = gates[:, 0:1] * buf[slot, 0]
    for k in range(1, TOP_K):
        ffn = ffn + gates[:, k:k + 1] * buf[slot, k]
    o_ref[...] = _layer_norm(alpha * h_ref[...] + ffn, lng_ref[...], lnb_ref[...])


def _combine(y_sorted, dest, gates, h, lng, lnb, alpha):
    t = h.shape[0]
    tt = _pick_tile(t, (128, 64, 16, 8))
    grid_spec = pltpu.PrefetchScalarGridSpec(
        num_scalar_prefetch=1,
        grid=(t // tt,),
        in_specs=[pl.BlockSpec(memory_space=pl.ANY),
                  pl.BlockSpec((tt, TOP_K), lambda i, d: (i, 0)),
                  pl.BlockSpec((tt, D_MODEL), lambda i, d: (i, 0)),
                  pl.BlockSpec((1, D_MODEL), lambda i, d: (0, 0)),
                  pl.BlockSpec((1, D_MODEL), lambda i, d: (0, 0))],
        out_specs=pl.BlockSpec((tt, D_MODEL), lambda i, d: (i, 0)),
        scratch_shapes=[pltpu.VMEM((2, TOP_K, tt, D_MODEL), F32), pltpu.SemaphoreType.DMA((2,))],
    )
    return pl.pallas_call(
        functools.partial(_combine_kernel, tt=tt, alpha=alpha),
        grid_spec=grid_spec,
        out_shape=jax.ShapeDtypeStruct((t, D_MODEL), F32),
        compiler_params=_cparams(("arbitrary",)),
        name="moe_combine",
    )(dest, y_sorted, gates, h, lng, lnb)


def _moe_plan(eidx, rank, counts, n_tiles, expert0):
    tm = MOE_TILE
    cnt = counts[0, :N_EXPERTS]
    padded = (cnt + tm - 1) // tm * tm
    ends = jnp.cumsum(padded)
    starts = ends - padded
    dest = (starts[eidx] + rank).reshape(-1).astype(jnp.int32)
    tile_start = jnp.arange(n_tiles, dtype=jnp.int32) * tm
    tile_expert = jnp.sum((tile_start[:, None] >= ends[None, :]).astype(jnp.int32), axis=1)
    tile_expert = (jnp.minimum(tile_expert, N_EXPERTS - 1) + expert0).astype(jnp.int32)
    n_used = (ends[-1] // tm).astype(jnp.int32).reshape(1)
    last_tile = (ends // tm - 1).astype(jnp.int32)
    has_rows = (padded > 0).astype(jnp.int32)
    return dest, tile_expert, n_used, last_tile, has_rows


def _layer(x2d, lw, shared, layer, depth, dims, alpha, kv_prev):
    nb_p, seq_p, nb_s, seq_s = dims
    t_p = nb_p * seq_p
    t_s = nb_s * seq_s
    page_table = shared["page_table"]
    page0 = layer * shared["n_pool"]
    proj = _project(x2d, lw["wt_main"])
    kbuf, vbuf = _kv_prompt(x2d, lw["wt_kv"], layer, depth, nb_p, seq_p, kv_prev)
    kv_s = _project(x2d, lw["wt_kv"], row0=t_p, rows=t_s)
    logft_p, cum_p, cumt_p = _fox_gate_prompt(x2d, lw["wt_f"], lw["b_f"], nb_p, seq_p)
    oa_p, s_p = _hgrn(proj, lw["lb"], lw["hgrn_g"], None, 0, 0, nb_p, seq_p)
    oa_s, s_s = _hgrn(proj, lw["lb"], lw["hgrn_g"], shared["state"], layer * nb_s, t_p, nb_s, seq_s)
    ob_p = _fox_prompt(proj, kbuf, vbuf, layer, cum_p, cumt_p, nb_p, seq_p)
    ob_s, logft_s = _fox_sample(proj, kv_s, x2d, lw["wt_f"], lw["b_f"], shared["cache_kt"], shared["cache_vt"],
                                shared["cache_logf_t"], page_table, page0, t_p, nb_s, seq_s)
    h, eidx, gates, rank, counts = _mix(oa_p, oa_s, ob_p, ob_s, proj, x2d, lw["w_a"], lw["w_b"], lw["w_o"],
                                        lw["ln_mix_g"], lw["ln_mix_b"], lw["wr_hi"], lw["wr_lo"], lw["b_r"], alpha)
    t = x2d.shape[0]
    n_tiles = (t * TOP_K + N_EXPERTS * (MOE_TILE - 1)) // MOE_TILE + 1
    dest, tile_expert, n_used, last_tile, has_rows = _moe_plan(eidx, rank, counts, n_tiles, layer * N_EXPERTS)
    x_sorted = _dispatch(h, dest, last_tile, has_rows, n_used, n_tiles)
    y_sorted = _experts(x_sorted, tile_expert, n_used,
                        shared["w_gate_up"], shared["b_gate_up"], shared["w_down"], shared["b_down"])
    y = _combine(y_sorted, dest, gates, h, lw["ln_ffn_g"], lw["ln_ffn_b"], alpha)
    return y, (kbuf, vbuf), kv_s, logft_p, logft_s, s_p, s_s


def kernel(x_prompt, x_sample, cache_k, cache_v, cache_logf, state_hgrn, page_table, w_in, b_fox_f, lb_logits,
           hgrn_norm_g, w_branch_a, w_branch_b, w_out, ln_mix_g, ln_mix_b, w_router, b_router, w_gate_up,
           b_gate_up, w_down, b_down, ln_ffn_g, ln_ffn_b):
    depth = w_in.shape[0]
    nb_p, seq_p, d = x_prompt.shape
    nb_s, seq_s, _ = x_sample.shape
    t_p, t_s = nb_p * seq_p, nb_s * seq_s
    alpha = (2 * depth) ** 0.25
    n_pool, page = cache_k.shape[1], cache_k.shape[2]
    dff2 = w_gate_up.shape[-1]

    pl_soft = jax.nn.softmax(lb_logits.astype(F32), axis=0)
    lower_bounds = jnp.cumsum(pl_soft, axis=0) - pl_soft[0:1]

    x2d = jnp.concatenate([x_prompt.reshape(t_p, d), x_sample.reshape(t_s, d)], axis=0)
    shared = {
        "page_table": page_table, "n_pool": n_pool,
        "cache_kt": cache_k.transpose(0, 1, 3, 4, 2).reshape(depth * n_pool, D_MODEL, page),
        "cache_vt": cache_v.transpose(0, 1, 3, 4, 2).reshape(depth * n_pool, D_MODEL, page),
        "cache_logf_t": cache_logf.transpose(0, 1, 3, 2).reshape(depth * n_pool, B_HEADS, page),
        "state": state_hgrn.reshape(depth * nb_s, A_HEADS, A_DK, A_DK),
        "w_gate_up": w_gate_up.reshape(depth * N_EXPERTS, d, dff2),
        "b_gate_up": b_gate_up.reshape(depth * N_EXPERTS, 1, dff2),
        "w_down": w_down.reshape(depth * N_EXPERTS, dff2 // 2, d),
        "b_down": b_down.reshape(depth * N_EXPERTS, 1, d),
    }
    wt_in = w_in.transpose(0, 2, 1)
    c_k = 5 * D_MODEL
    c_f = 7 * D_MODEL
    pad_r = LANES - N_EXPERTS
    kv_prev = None
    fp, sp, ks, vs, fs, ss = [], [], [], [], [], []
    for l in range(depth):
        wt = wt_in[l]
        wr_t = jnp.pad(w_router[l].T, ((0, pad_r), (0, 0)))
        wr_hi = wr_t.astype(BF16)
        lw = {
            "wt_main": jnp.concatenate([wt[:c_k], wt[c_f + B_HEADS:]], axis=0).astype(BF16),
            "wt_kv": wt[c_k:c_f].astype(BF16),
            "wt_f": wt[c_f:c_f + B_HEADS].astype(BF16),
            "b_f": b_fox_f[l],
            "lb": lower_bounds[l].reshape(1, d),
            "hgrn_g": hgrn_norm_g[l].reshape(1, A_DK),
            "w_a": w_branch_a[l].astype(BF16), "w_b": w_branch_b[l].astype(BF16), "w_o": w_out[l].astype(BF16),
            "ln_mix_g": ln_mix_g[l].reshape(1, d), "ln_mix_b": ln_mix_b[l].reshape(1, d),
            "ln_ffn_g": ln_ffn_g[l].reshape(1, d), "ln_ffn_b": ln_ffn_b[l].reshape(1, d),
            "b_r": jnp.pad(b_router[l], (0, pad_r), constant_values=MASK_VALUE).reshape(1, LANES),
            "wr_hi": wr_hi, "wr_lo": (wr_t - wr_hi.astype(F32)).astype(BF16),
        }
        x2d, kv_prev, kv_s, logft_p, logft_s, s_p, s_s = _layer(
            x2d, lw, shared, l, depth, (nb_p, seq_p, nb_s, seq_s), alpha, kv_prev)
        ks.append(kv_s[:, :D_MODEL].reshape(nb_s, seq_s, B_HEADS, B_DH))
        vs.append(kv_s[:, D_MODEL:].reshape(nb_s, seq_s, B_HEADS, B_DH))
        fp.append(logft_p)
        fs.append(logft_s)
        sp.append(s_p)
        ss.append(s_s)
    y_p = x2d[:t_p].reshape(nb_p, seq_p, d)
    y_s = x2d[t_p:].reshape(nb_s, seq_s, d)
    kbuf, vbuf = kv_prev
    k_p = kbuf.reshape(depth, nb_p, B_HEADS, B_DH, seq_p).transpose(0, 1, 4, 2, 3)
    v_p = vbuf.reshape(depth, nb_p, B_HEADS, B_DH, seq_p).transpose(0, 1, 4, 2, 3)
    f_p = jnp.stack(fp).transpose(0, 1, 3, 2)
    f_s = jnp.stack(fs).transpose(0, 1, 3, 2)
    return (y_p, y_s, k_p, v_p, f_p, jnp.stack(sp), jnp.stack(ks), jnp.stack(vs), f_s, jnp.stack(ss))
```

```python
import functools

import jax
import jax.numpy as jnp
from jax import lax
from jax.experimental import pallas as pl
from jax.experimental.pallas import tpu as pltpu

F32 = jnp.float32
BF16 = jnp.bfloat16

D_MODEL = 1024
A_HEADS = 8
A_DK = 128
B_HEADS = 16
B_DH = 64
N_EXPERTS = 32
TOP_K = 4
SWIGLU_LIMIT = 7.0
SWIGLU_ALPHA = 1.702
LN_EPS = 1e-5
RMS_EPS = 1e-6
MASK_VALUE = -1e30
LOG2E = 1.4426950408889634
LANES = 128
HGRN_CHUNK = 64
HGRN_SAFE_DECAY = 80.0
VMEM_LIMIT = 56 * 1024 * 1024
PROJ_Q0 = 4 * D_MODEL
PROJ_G0 = 5 * D_MODEL
MOE_TILE = 512


def _pick_tile(n, candidates):
    for c in candidates:
        if n % c == 0:
            return c
    return n


def _cparams(sem, vmem=None):
    return pltpu.CompilerParams(dimension_semantics=sem, vmem_limit_bytes=vmem or VMEM_LIMIT)


def _split3(x):
    hi = x.astype(BF16)
    r = x - hi.astype(F32)
    mid = r.astype(BF16)
    lo = (r - mid.astype(F32)).astype(BF16)
    return hi, mid, lo


def _dot(a, b):
    return jnp.dot(a, b, preferred_element_type=F32)


def _dot_nt(a, b):
    return lax.dot_general(a, b, (((1,), (1,)), ((), ())), preferred_element_type=F32)


def _dot_tn(a, b):
    return lax.dot_general(a, b, (((0,), (0,)), ((), ())), preferred_element_type=F32)


def _tri(n, upper=False):
    r = lax.broadcasted_iota(jnp.int32, (n, n), 0)
    c = lax.broadcasted_iota(jnp.int32, (n, n), 1)
    return ((r <= c) if upper else (c <= r)).astype(BF16)


def _cumsum_rows(x):
    tri = _tri(x.shape[0])
    hi, mid, lo = _split3(x)
    return _dot(tri, hi) + _dot(tri, mid) + _dot(tri, lo)


def _cumsum_lanes(x):
    tri = _tri(x.shape[1], upper=True)
    hi, mid, lo = _split3(x)
    return _dot(hi, tri) + _dot(mid, tri) + _dot(lo, tri)


def _log_sigmoid(z):
    return jnp.minimum(z, 0.0) - jnp.log(1.0 + jnp.exp(-jnp.abs(z)))


def _sigmoid(z):
    return 1.0 / (1.0 + jnp.exp(-z))


def _silu(z):
    return z * _sigmoid(z)


def _layer_norm(x, g, b):
    mu = jnp.mean(x, axis=-1, keepdims=True)
    xc = x - mu
    var = jnp.mean(xc * xc, axis=-1, keepdims=True)
    return xc * lax.rsqrt(var + LN_EPS) * g + b


def _drop_ref(kern, idx):
    def wrapped(*refs):
        return kern(*(refs[:idx] + refs[idx + 1:]))
    return wrapped


def _proj_kernel(x_ref, wt_ref, o_ref):
    o_ref[...] = _dot_nt(x_ref[...].astype(BF16), wt_ref[...])


def _project(x2d, wt, row0=0, rows=None):
    d = x2d.shape[1]
    t = x2d.shape[0] if rows is None else rows
    n = wt.shape[0]
    tm = _pick_tile(t, (1280, 640, 512, 256))
    tn = _pick_tile(n, (1024, 512, 256, 128))
    rb0 = row0 // tm
    return pl.pallas_call(
        _proj_kernel,
        grid=(t // tm, n // tn),
        in_specs=[pl.BlockSpec((tm, d), lambda i, j: (rb0 + i, 0)),
                  pl.BlockSpec((tn, d), lambda i, j: (j, 0))],
        out_specs=pl.BlockSpec((tm, tn), lambda i, j: (i, j)),
        out_shape=jax.ShapeDtypeStruct((t, n), F32),
        compiler_params=_cparams(("parallel", "parallel")),
        name="in_proj",
    )(x2d, wt)


def _kv_prompt_kernel(x_ref, wt_ref, k_ref, v_ref, *, layer, fill_others):
    def compute():
        kv = _dot_nt(wt_ref[...], x_ref[...].astype(BF16))
        k_ref[0, 0] = kv[:D_MODEL]
        v_ref[0, 0] = kv[D_MODEL:]

    if not fill_others:
        compute()
        return
    s = pl.program_id(0)
    pl.when(s == layer)(compute)

    @pl.when(s != layer)
    def _():
        k_ref[...] = jnp.zeros_like(k_ref)
        v_ref[...] = jnp.zeros_like(v_ref)


def _kv_prompt(x2d, wt_kv, layer, depth, nb, seq, prev):
    d = x2d.shape[1]
    tl = _pick_tile(seq, (512, 256, 128))
    nl = seq // tl
    first = prev is None
    in_specs = [pl.BlockSpec((tl, d), lambda s, b, i: (b * nl + i, 0)),
                pl.BlockSpec((2 * D_MODEL, d), lambda s, b, i: (0, 0))]
    args = [x2d, wt_kv]
    kern = functools.partial(_kv_prompt_kernel, layer=layer, fill_others=first)
    aliases = {}
    if first:
        ospec = pl.BlockSpec((1, 1, D_MODEL, tl), lambda s, b, i: (s, b, 0, i))
    else:
        in_specs += [pl.BlockSpec(memory_space=pl.ANY)] * 2
        args += list(prev)
        aliases = {2: 0, 3: 1}
        kern = _drop_ref(_drop_ref(kern, 2), 3)
        ospec = pl.BlockSpec((1, 1, D_MODEL, tl), lambda s, b, i: (layer, b, 0, i))
    out = jax.ShapeDtypeStruct((depth, nb, D_MODEL, seq), F32)
    return pl.pallas_call(
        kern,
        grid=(depth if first else 1, nb, nl),
        in_specs=in_specs,
        out_specs=[ospec, ospec],
        out_shape=[out, out],
        input_output_aliases=aliases,
        compiler_params=_cparams(("parallel", "parallel", "parallel")),
        name="kv_prompt",
    )(*args)


def _fox_gate_prompt_kernel(x_ref, wft_ref, bf_ref, bft_ref, logft_ref, cum_ref, cumt_ref, carry_ref, carryt_ref):
    i = pl.program_id(1)

    @pl.when(i == 0)
    def _():
        carry_ref[...] = jnp.zeros_like(carry_ref)
        carryt_ref[...] = jnp.zeros_like(carryt_ref)

    xb = x_ref[...].astype(BF16)
    logf = _log_sigmoid(_dot_nt(xb, wft_ref[...]) + bf_ref[...])
    logft = _log_sigmoid(_dot_nt(wft_ref[...], xb) + bft_ref[...])
    tl = logf.shape[0]
    cs = _cumsum_rows(logf) + carry_ref[...]
    cst = _cumsum_lanes(logft) + carryt_ref[...]
    carry_ref[...] = cs[tl - 1:tl, :]
    carryt_ref[...] = cst[:, tl - 1:tl]
    logft_ref[0] = logft
    cum_ref[0] = cs
    cumt_ref[0] = cst


def _fox_gate_prompt(x2d, wft, bf, nb, seq):
    d = x2d.shape[1]
    tl = _pick_tile(seq, (256, 128))
    nl = seq // tl
    out_t = jax.ShapeDtypeStruct((nb, B_HEADS, seq), F32)
    tspec = pl.BlockSpec((1, B_HEADS, tl), lambda b, i: (b, 0, i))
    return pl.pallas_call(
        _fox_gate_prompt_kernel,
        grid=(nb, nl),
        in_specs=[pl.BlockSpec((tl, d), lambda b, i: (b * nl + i, 0)),
                  pl.BlockSpec((B_HEADS, d), lambda b, i: (0, 0)),
                  pl.BlockSpec((1, B_HEADS), lambda b, i: (0, 0)),
                  pl.BlockSpec((B_HEADS, 1), lambda b, i: (0, 0))],
        out_specs=[tspec, pl.BlockSpec((1, tl, B_HEADS), lambda b, i: (b, i, 0)), tspec],
        out_shape=[out_t, jax.ShapeDtypeStruct((nb, seq, B_HEADS), F32), out_t],
        scratch_shapes=[pltpu.VMEM((1, B_HEADS), F32), pltpu.VMEM((B_HEADS, 1), F32)],
        compiler_params=_cparams(("parallel", "arbitrary")),
        name="fox_gate_prompt",
    )(x2d, wft, bf.reshape(1, B_HEADS), bf.reshape(B_HEADS, 1))


def _hgrn_kernel(*refs, chunk, n_chunks, has_s0):
    if has_s0:
        (aq_ref, af_ref, ai_ref, ag_ref, lb_ref, g_ref, s0_ref, o_ref, snew_ref,
         st_ref, a_ref, gs_ref, ks_ref, qt_ref, kt_ref, vb_ref) = refs
    else:
        (aq_ref, af_ref, ai_ref, ag_ref, lb_ref, g_ref, o_ref, snew_ref,
         st_ref, a_ref, gs_ref, ks_ref, qt_ref, kt_ref, vb_ref) = refs
        s0_ref = None
    i = pl.program_id(1)
    c = chunk
    rows = c * n_chunks

    @pl.when(i == 0)
    def _():
        for h in range(A_HEADS):
            if has_s0:
                st_ref[h] = s0_ref[0, h].T
            else:
                st_ref[h] = jnp.zeros((A_DK, A_DK), F32)

    lbv = lb_ref[...]
    gv = g_ref[...]
    scale = A_DK ** -0.5

    z = af_ref[...]
    logf = jnp.log(lbv + (1.0 - lbv) * _sigmoid(z))
    kk = (1.0 - lbv) * _sigmoid(-z)
    rr = lax.broadcasted_iota(jnp.int32, (rows, rows), 0)
    cc = lax.broadcasted_iota(jnp.int32, (rows, rows), 1)
    tri = ((rr // c == cc // c) & (cc <= rr)).astype(BF16)
    hi, mid, lo = _split3(logf)
    g_cum = _dot(tri, hi) + _dot(tri, mid) + _dot(tri, lo)
    gs_ref[...] = g_cum
    ks_ref[...] = kk
    qt_ref[...] = (_silu(aq_ref[...]) * scale * jnp.exp(g_cum)).astype(BF16)
    kt_ref[...] = (kk * jnp.exp(-g_cum)).astype(BF16)
    vb_ref[...] = ai_ref[...].astype(BF16)
    a_ref[...] = jnp.zeros_like(a_ref)

    row = lax.broadcasted_iota(jnp.int32, (c, c), 0)
    col = lax.broadcasted_iota(jnp.int32, (c, c), 1)
    causal = col <= row

    safes = []
    for ci in range(n_chunks):
        r = slice(ci * c, (ci + 1) * c)
        safe = jnp.min(gs_ref[(ci + 1) * c - 1:(ci + 1) * c, :]) >= -HGRN_SAFE_DECAY
        safes.append(safe)

        @pl.when(jnp.logical_not(safe))
        def _():
            q = _silu(aq_ref[r, :]) * scale
            g_c = gs_ref[r, :]

            def col_j(j, carry):
                gj = gs_ref[pl.ds(ci * c + j, 1), :]
                kj = ks_ref[pl.ds(ci * c + j, 1), :]
                w = q * kj * jnp.exp(jnp.minimum(g_c - gj, 0.0))
                for h in range(A_HEADS):
                    hs = slice(h * A_DK, (h + 1) * A_DK)
                    cj = jnp.sum(w[:, hs], axis=1, keepdims=True)
                    a_ref[ci, h] = a_ref[ci, h] + jnp.where(col == j, cj, 0.0)
                return carry

            lax.fori_loop(0, c, col_j, 0)

    for ci in range(n_chunks):
        r = slice(ci * c, (ci + 1) * c)
        safe = safes[ci]
        g_c = gs_ref[r, :]
        gend = g_c[c - 1:c, :]
        khat = (ks_ref[r, :] * jnp.exp(gend - g_c)).astype(BF16)
        egend = jnp.exp(gend)
        for h in range(A_HEADS):
            hs = slice(h * A_DK, (h + 1) * A_DK)
            qt = qt_ref[r, hs]
            vb = vb_ref[r, hs]
            a_fast = _dot_nt(qt, kt_ref[r, hs])
            att = jnp.where(causal, jnp.where(safe, a_fast, a_ref[ci, h]), 0.0).astype(BF16)
            st = st_ref[h]
            o = _dot(att, vb) + _dot_nt(qt, st.astype(BF16))
            st_ref[h] = st * egend[:, hs] + _dot_tn(vb, khat[:, hs])
            ms = jnp.mean(o * o, axis=-1, keepdims=True)
            o = o * lax.rsqrt(ms + RMS_EPS) * gv
            o_ref[r, hs] = o * _silu(ag_ref[r, hs])

    @pl.when(i == pl.num_programs(1) - 1)
    def _():
        for h in range(A_HEADS):
            snew_ref[0, h] = st_ref[h].T


def _hgrn(proj, lb, g, s0, s0_base, row0, nb, seq):
    chunk = min(HGRN_CHUNK, seq)
    lb_rows = _pick_tile(seq, (256, chunk))
    nl = seq // lb_rows
    rb0 = row0 // lb_rows
    has_s0 = s0 is not None
    in_specs = [pl.BlockSpec((lb_rows, D_MODEL), functools.partial(lambda b, i, k: (rb0 + b * nl + i, k), k=k))
                for k in range(4)]
    in_specs += [pl.BlockSpec((1, D_MODEL), lambda b, i: (0, 0)),
                 pl.BlockSpec((1, A_DK), lambda b, i: (0, 0))]
    args = [proj, proj, proj, proj, lb, g]
    if has_s0:
        in_specs.append(pl.BlockSpec((1, A_HEADS, A_DK, A_DK), lambda b, i: (s0_base + b, 0, 0, 0)))
        args.append(s0)
    kern = functools.partial(_hgrn_kernel, chunk=chunk, n_chunks=lb_rows // chunk, has_s0=has_s0)
    return pl.pallas_call(
        kern,
        grid=(nb, nl),
        in_specs=in_specs,
        out_specs=[pl.BlockSpec((lb_rows, D_MODEL), lambda b, i: (b * nl + i, 0)),
                   pl.BlockSpec((1, A_HEADS, A_DK, A_DK), lambda b, i: (b, 0, 0, 0))],
        out_shape=[jax.ShapeDtypeStruct((nb * seq, D_MODEL), F32),
                   jax.ShapeDtypeStruct((nb, A_HEADS, A_DK, A_DK), F32)],
        scratch_shapes=[pltpu.VMEM((A_HEADS, A_DK, A_DK), F32),
                        pltpu.VMEM((lb_rows // chunk, A_HEADS, chunk, chunk), F32),
                        pltpu.VMEM((lb_rows, D_MODEL), F32), pltpu.VMEM((lb_rows, D_MODEL), F32),
                        pltpu.VMEM((lb_rows, D_MODEL), BF16), pltpu.VMEM((lb_rows, D_MODEL), BF16),
                        pltpu.VMEM((lb_rows, D_MODEL), BF16)],
        compiler_params=_cparams(("parallel", "arbitrary")),
        name="hgrn",
    )(*args)


def _fox_prompt_kernel(q_ref, k_ref, v_ref, cum_ref, cumt_ref, o_ref, kn_ref, qt_ref, vt_ref, s_ref, p_ref,
                       *, tq, tk):
    j = pl.program_id(1)
    seq = q_ref.shape[0]
    nq = seq // tq
    hpair = (2 * j, 2 * j + 1)

    kn_ref[:, :LANES] = k_ref[0, 0].T.astype(BF16)
    h16 = lax.broadcasted_iota(jnp.int32, (B_HEADS, LANES), 0)
    l16 = lax.broadcasted_iota(jnp.int32, (B_HEADS, LANES), 1)
    kaug = jnp.zeros((seq, LANES), F32)
    for part, term in enumerate(_split3(cum_ref[0] * LOG2E)):
        sel = jnp.where(((h16 == hpair[0]) & (l16 == part)) | ((h16 == hpair[1]) & (l16 == 3 + part)), -1.0, 0.0)
        kaug = kaug + _dot(term, sel.astype(BF16))
    lane_s = lax.broadcasted_iota(jnp.int32, (seq, LANES), 1)
    kn_ref[:, LANES:] = (kaug + jnp.where((lane_s >= 6) & (lane_s < 9), 1.0, 0.0)).astype(BF16)

    qt = (q_ref[...] * (B_DH ** -0.5 * LOG2E)).T
    row = lax.broadcasted_iota(jnp.int32, (LANES, seq), 0)
    r16 = lax.broadcasted_iota(jnp.int32, (LANES, B_HEADS), 0)
    c16 = lax.broadcasted_iota(jnp.int32, (LANES, B_HEADS), 1)
    cumt_terms = _split3(cumt_ref[0] * LOG2E)
    for h in range(2):
        qt_ref[h, :LANES, :] = jnp.where(row // B_DH == h, qt, 0.0).astype(BF16)
        qaug = jnp.where((row >= 3 * h) & (row < 3 * h + 3), 1.0, 0.0)
        for part, term in enumerate(cumt_terms):
            sel = jnp.where((r16 == 6 + part) & (c16 == hpair[h]), 1.0, 0.0)
            qaug = qaug + _dot(sel.astype(BF16), term)
        qt_ref[h, LANES:, :] = qaug.astype(BF16)
    vt_ref[...] = v_ref[0, 0].astype(BF16)

    kpos = lax.broadcasted_iota(jnp.int32, (tk, tq), 0)
    qpos = lax.broadcasted_iota(jnp.int32, (tk, tq), 1)
    kpq = tq // tk

    sub = tk // 8

    def fold(x):
        return x.reshape(sub, 8, tq)

    for qi in range(nq):
        q0 = qi * tq
        nfull = qi * kpq
        outs = []
        for h in range(2):
            qa = qt_ref[h, :, q0:q0 + tq]

            def scores(ki, mx, qa=qa, h=h):
                ks = pl.ds(pl.multiple_of(ki * tk, tk), tk)
                s = _dot(kn_ref[ks, :], qa)
                s_ref[h, ks, :] = s
                return jnp.maximum(mx, jnp.max(fold(s), axis=0))

            mx = jnp.full((8, tq), -jnp.inf, F32)
            if nfull:
                mx = lax.fori_loop(0, nfull, scores, mx, unroll=2)
            for dk in range(kpq):
                k0 = (nfull + dk) * tk
                s = _dot(kn_ref[k0:k0 + tk, :], qa)
                s = jnp.where(k0 + kpos <= q0 + qpos, s, MASK_VALUE)
                s_ref[h, k0:k0 + tk, :] = s
                mx = jnp.maximum(mx, jnp.max(fold(s), axis=0))
            m = jnp.max(mx, axis=0, keepdims=True)

            def probs(ki, lsum, m=m, h=h):
                ks = pl.ds(pl.multiple_of(ki * tk, tk), tk)
                p = jnp.exp2(s_ref[h, ks, :] - m)
                p_ref[h, ks, :] = p.astype(BF16)
                return lsum + jnp.sum(fold(p), axis=0)

            nk = nfull + kpq
            lsum = lax.fori_loop(0, nk, probs, jnp.zeros((8, tq), F32), unroll=2)
            l = jnp.sum(lsum, axis=0, keepdims=True)
            acc = _dot(vt_ref[h * B_DH:(h + 1) * B_DH, 0:nk * tk], p_ref[h, 0:nk * tk, :])
            outs.append(acc * (1.0 / l))
        o_ref[q0:q0 + tq, :] = jnp.concatenate(outs, axis=0).T


def _fox_prompt(proj, kbuf, vbuf, layer, cum, cum_t, nb, seq):
    t = nb * seq
    tq = _pick_tile(seq, (512, 256, 128))
    tk = min(tq, 256)
    npair = B_HEADS // 2
    c0 = PROJ_Q0 // LANES
    kvspec = pl.BlockSpec((1, 1, LANES, seq), lambda b, j: (layer, b, j, 0))
    return pl.pallas_call(
        functools.partial(_fox_prompt_kernel, tq=tq, tk=tk),
        grid=(nb, npair),
        in_specs=[pl.BlockSpec((seq, LANES), lambda b, j: (b, c0 + j)), kvspec, kvspec,
                  pl.BlockSpec((1, seq, B_HEADS), lambda b, j: (b, 0, 0)),
                  pl.BlockSpec((1, B_HEADS, seq), lambda b, j: (b, 0, 0))],
        out_specs=pl.BlockSpec((seq, LANES), lambda b, j: (b, j)),
        out_shape=jax.ShapeDtypeStruct((t, D_MODEL), F32),
        scratch_shapes=[pltpu.VMEM((seq, 2 * LANES), BF16), pltpu.VMEM((2, 2 * LANES, seq), BF16),
                        pltpu.VMEM((LANES, seq), BF16), pltpu.VMEM((2, seq, tq), F32),
                        pltpu.VMEM((2, seq, tq), BF16)],
        compiler_params=_cparams(("parallel", "parallel")),
        name="fox_prompt",
    )(proj, kbuf, vbuf, cum, cum_t)


def _fox_sample_kernel(*refs, pages_per_step, ns):
    pps = pages_per_step
    q_ref, kvn_ref, x_ref, wft_ref, bft_ref = refs[1:6]
    k_refs = refs[6:6 + pps]
    v_refs = refs[6 + pps:6 + 2 * pps]
    f_refs = refs[6 + 2 * pps:6 + 3 * pps]
    o_ref, logft_ref = refs[6 + 3 * pps:8 + 3 * pps]
    qbd_ref, fq_ref, cn_ref, past_ref, m_ref, l_ref, acc_ref = refs[8 + 3 * pps:]
    g = pl.program_id(1)
    nrow = B_HEADS * ns
    rh = lax.broadcasted_iota(jnp.int32, (nrow, D_MODEL), 0) // ns
    ch = lax.broadcasted_iota(jnp.int32, (nrow, D_MODEL), 1) // B_DH

    def expand_rows(x):
        n = x.shape[1]
        return jnp.broadcast_to(x[:, None, :], (B_HEADS, ns, n)).reshape(nrow, n)

    @pl.when(g == 0)
    def _():
        q = q_ref[...] * (B_DH ** -0.5)
        qrep = jnp.broadcast_to(q[None], (B_HEADS, ns, D_MODEL)).reshape(nrow, D_MODEL)
        qbd_ref[...] = jnp.where(rh == ch, qrep, 0.0).astype(BF16)
        logft = _log_sigmoid(_dot_nt(wft_ref[...], x_ref[...].astype(BF16)) + bft_ref[...])
        logft_ref[0] = logft
        cn = _cumsum_lanes(logft)
        cn_ref[...] = cn
        tsel = (lax.broadcasted_iota(jnp.int32, (nrow, ns), 0) % ns
                == lax.broadcasted_iota(jnp.int32, (nrow, ns), 1))
        fq_ref[...] = jnp.sum(jnp.where(tsel, expand_rows(cn), 0.0), axis=1, keepdims=True)
        past_ref[...] = jnp.zeros_like(past_ref)
        m_ref[...] = jnp.full_like(m_ref, -jnp.inf)
        l_ref[...] = jnp.zeros_like(l_ref)
        acc_ref[...] = jnp.zeros_like(acc_ref)

    def update(s, pv):
        m_old = m_ref[...]
        m_new = jnp.maximum(m_old, jnp.max(s, axis=1, keepdims=True))
        alpha = jnp.exp(m_old - m_new)
        p = jnp.exp(s - m_new)
        l_ref[...] = alpha * l_ref[...] + jnp.sum(p, axis=1, keepdims=True)
        acc_ref[...] = alpha * acc_ref[...] + pv(p.astype(BF16))
        m_ref[...] = m_new

    page = k_refs[0].shape[2]
    lf = jnp.concatenate([f_refs[pi][0] for pi in range(pps)], axis=0)
    cs = _cumsum_lanes(lf)
    offset = past_ref[...]
    s_parts = []
    for pi in range(pps):
        cs_p = cs[pi * B_HEADS:(pi + 1) * B_HEADS, :] + offset
        offset = cs_p[:, page - 1:]
        kt = k_refs[pi][0].astype(BF16)
        s_parts.append(_dot(qbd_ref[...], kt) + fq_ref[...] - expand_rows(cs_p))
    past_ref[...] = offset

    def pv_pages(p):
        out = None
        for pi in range(pps):
            term = _dot_nt(p[:, pi * page:(pi + 1) * page], v_refs[pi][0].astype(BF16))
            out = term if out is None else out + term
        return out

    update(jnp.concatenate(s_parts, axis=1), pv_pages)

    @pl.when(g == pl.num_programs(1) - 1)
    def _():
        kb = kvn_ref[:, :D_MODEL].astype(BF16)
        vb = kvn_ref[:, D_MODEL:].astype(BF16)
        fk = expand_rows(cn_ref[...] + past_ref[...])
        s = _dot_nt(qbd_ref[...], kb) + fq_ref[...] - fk
        tq = lax.broadcasted_iota(jnp.int32, (nrow, ns), 0) % ns
        tk = lax.broadcasted_iota(jnp.int32, (nrow, ns), 1)
        s = jnp.where(tq >= tk, s, MASK_VALUE)
        update(s, lambda p: _dot(p, vb))
        o = acc_ref[...] / l_ref[...]
        for h in range(B_HEADS):
            o_ref[:, h * B_DH:(h + 1) * B_DH] = o[h * ns:(h + 1) * ns, h * B_DH:(h + 1) * B_DH]


def _fox_sample(proj, kv_new, x2d, wft, bf, cache_kt, cache_vt, cache_logf_t, page_table, page0, row0, nb, ns):
    d = x2d.shape[1]
    n_pages = page_table.shape[1]
    page = cache_kt.shape[2]
    pps = _pick_tile(n_pages, (8, 4, 2, 1))
    ng = n_pages // pps
    rb0 = row0 // ns
    nrow = B_HEADS * ns

    def page_spec(pi, rows):
        return pl.BlockSpec((1, rows, page), lambda b, g, pt: (page0 + pt[b, g * pps + pi], 0, 0))

    in_specs = [pl.BlockSpec((ns, D_MODEL), lambda b, g, pt: (rb0 + b, PROJ_Q0 // D_MODEL)),
                pl.BlockSpec((ns, 2 * D_MODEL), lambda b, g, pt: (b, 0)),
                pl.BlockSpec((ns, d), lambda b, g, pt: (rb0 + b, 0)),
                pl.BlockSpec((B_HEADS, d), lambda b, g, pt: (0, 0)),
                pl.BlockSpec((B_HEADS, 1), lambda b, g, pt: (0, 0))]
    in_specs += [page_spec(pi, D_MODEL) for pi in range(pps)] * 2
    in_specs += [page_spec(pi, B_HEADS) for pi in range(pps)]
    args = [proj, kv_new, x2d, wft, bf.reshape(B_HEADS, 1)]
    args += [cache_kt] * pps + [cache_vt] * pps + [cache_logf_t] * pps
    grid_spec = pltpu.PrefetchScalarGridSpec(
        num_scalar_prefetch=1,
        grid=(nb, ng),
        in_specs=in_specs,
        out_specs=[pl.BlockSpec((ns, D_MODEL), lambda b, g, pt: (b, 0)),
                   pl.BlockSpec((1, B_HEADS, ns), lambda b, g, pt: (b, 0, 0))],
        scratch_shapes=[pltpu.VMEM((nrow, D_MODEL), BF16), pltpu.VMEM((nrow, 1), F32),
                        pltpu.VMEM((B_HEADS, ns), F32), pltpu.VMEM((B_HEADS, 1), F32),
                        pltpu.VMEM((nrow, 1), F32), pltpu.VMEM((nrow, 1), F32),
                        pltpu.VMEM((nrow, D_MODEL), F32)],
    )
    return pl.pallas_call(
        functools.partial(_fox_sample_kernel, pages_per_step=pps, ns=ns),
        grid_spec=grid_spec,
        out_shape=[jax.ShapeDtypeStruct((nb * ns, D_MODEL), F32),
                   jax.ShapeDtypeStruct((nb, B_HEADS, ns), F32)],
        compiler_params=_cparams(("parallel", "arbitrary")),
        name="fox_sample",
    )(page_table, *args)


def _mix_kernel(oap_ref, oas_ref, obp_ref, obs_ref, ga_ref, gb_ref, x_ref, wa_ref, wb_ref, wo_ref, lng_ref,
                lnb_ref, wrh_ref, wrl_ref, br_ref, h_ref, eidx_ref, gate_ref, rank_ref, cnt_ref, carry_ref,
                *, alpha, n_prompt_tiles):
    i = pl.program_id(0)

    @pl.when(i == 0)
    def _():
        carry_ref[...] = jnp.zeros_like(carry_ref)

    is_prompt = i < n_prompt_tiles
    oa = jnp.where(is_prompt, oap_ref[...], oas_ref[...])
    ob = jnp.where(is_prompt, obp_ref[...], obs_ref[...])
    ya = _dot(oa.astype(BF16), wa_ref[...])
    yb = _dot(ob.astype(BF16), wb_ref[...])
    m = _sigmoid(ga_ref[...]) * ya + _sigmoid(gb_ref[...]) * yb
    mix = _dot(m.astype(BF16), wo_ref[...])
    h = _layer_norm(alpha * x_ref[...] + mix, lng_ref[...], lnb_ref[...])
    h_ref[...] = h

    h_hi = h.astype(BF16)
    h_lo = (h - h_hi.astype(F32)).astype(BF16)
    logits = (_dot_nt(h_hi, wrh_ref[...]) + _dot_nt(h_lo, wrh_ref[...]) + _dot_nt(h_hi, wrl_ref[...])
              + br_ref[...])
    tm = logits.shape[0]
    lane = lax.broadcasted_iota(jnp.int32, (tm, LANES), 1)
    work = logits
    idxs, vals = [], []
    for _ in range(TOP_K):
        mx = jnp.max(work, axis=1, keepdims=True)
        ix = jnp.min(jnp.where(work == mx, lane, LANES), axis=1, keepdims=True)
        idxs.append(ix)
        vals.append(mx)
        work = jnp.where(lane == ix, -jnp.inf, work)
    exps = [jnp.exp(v - vals[0]) for v in vals]
    denom = exps[0] + exps[1] + exps[2] + exps[3]
    multihot = jnp.zeros((tm, LANES), F32)
    for ix in idxs:
        multihot = multihot + jnp.where(lane == ix, 1.0, 0.0)
    r = lax.broadcasted_iota(jnp.int32, (tm, tm), 0)
    c = lax.broadcasted_iota(jnp.int32, (tm, tm), 1)
    before = _dot((c < r).astype(BF16), multihot.astype(BF16)) + carry_ref[...]
    carry_ref[...] = carry_ref[...] + jnp.sum(multihot, axis=0, keepdims=True)
    col4 = lax.broadcasted_iota(jnp.int32, (tm, TOP_K), 1)
    eidx = jnp.zeros((tm, TOP_K), jnp.int32)
    gates = jnp.zeros((tm, TOP_K), F32)
    ranks = jnp.zeros((tm, TOP_K), F32)
    for k in range(TOP_K):
        rk = jnp.sum(jnp.where(lane == idxs[k], before, 0.0), axis=1, keepdims=True)
        eidx = jnp.where(col4 == k, idxs[k], eidx)
        gates = jnp.where(col4 == k, exps[k] / denom, gates)
        ranks = jnp.where(col4 == k, rk, ranks)
    eidx_ref[...] = eidx
    gate_ref[...] = gates
    rank_ref[...] = ranks.astype(jnp.int32)
    cnt_ref[...] = carry_ref[...].astype(jnp.int32)


def _mix(oa_p, oa_s, ob_p, ob_s, proj, x2d, wa, wb, wo, lng, lnb, wrh, wrl, br, alpha):
    t = x2d.shape[0]
    t_p, t_s = oa_p.shape[0], oa_s.shape[0]
    tm = next(c for c in (256, 128, 64, 32, 16, 8) if t_p % c == 0 and t_s % c == 0)
    npt = t_p // tm
    row = lambda i: (i, 0)
    prow = lambda i: (jnp.minimum(i, npt - 1), 0)
    srow = lambda i: (jnp.maximum(i - npt, 0), 0)
    const = lambda i: (0, 0)
    wspec = pl.BlockSpec((D_MODEL, D_MODEL), const)
    vspec = pl.BlockSpec((1, D_MODEL), const)
    return pl.pallas_call(
        functools.partial(_mix_kernel, alpha=alpha, n_prompt_tiles=npt),
        grid=(t // tm,),
        in_specs=[pl.BlockSpec((tm, D_MODEL), prow), pl.BlockSpec((tm, D_MODEL), srow),
                  pl.BlockSpec((tm, D_MODEL), prow), pl.BlockSpec((tm, D_MODEL), srow),
                  pl.BlockSpec((tm, D_MODEL), lambda i: (i, PROJ_G0 // D_MODEL)),
                  pl.BlockSpec((tm, D_MODEL), lambda i: (i, PROJ_G0 // D_MODEL + 1)),
                  pl.BlockSpec((tm, D_MODEL), row), wspec, wspec, wspec, vspec, vspec,
                  pl.BlockSpec((LANES, D_MODEL), const), pl.BlockSpec((LANES, D_MODEL), const),
                  pl.BlockSpec((1, LANES), const)],
        out_specs=[pl.BlockSpec((tm, D_MODEL), row), pl.BlockSpec((tm, TOP_K), row),
                   pl.BlockSpec((tm, TOP_K), row), pl.BlockSpec((tm, TOP_K), row),
                   pl.BlockSpec((1, LANES), const)],
        out_shape=[jax.ShapeDtypeStruct((t, D_MODEL), F32), jax.ShapeDtypeStruct((t, TOP_K), jnp.int32),
                   jax.ShapeDtypeStruct((t, TOP_K), F32), jax.ShapeDtypeStruct((t, TOP_K), jnp.int32),
                   jax.ShapeDtypeStruct((1, LANES), jnp.int32)],
        scratch_shapes=[pltpu.VMEM((1, LANES), F32)],
        compiler_params=_cparams(("arbitrary",)),
        name="mix_router",
    )(oa_p, oa_s, ob_p, ob_s, proj, proj, x2d, wa, wb, wo, lng, lnb, wrh, wrl, br)


def _dispatch_kernel(dest_ref, last_ref, has_ref, nused_ref, h_ref, x_hbm, stage, zeros, sems, zsem,
                     *, tt, n_tiles, min_used):
    i = pl.program_id(0)
    n = pl.num_programs(0)
    slot = i % 2
    tm = MOE_TILE

    @pl.when(i == 0)
    def _():
        zeros[...] = jnp.zeros_like(zeros)

        def fill(tile):
            return pltpu.make_async_copy(zeros, x_hbm.at[pl.ds(tile * tm, tm), :], zsem)

        for e in range(N_EXPERTS):
            pl.when(has_ref[e] > 0)(lambda e=e: fill(last_ref[e]).start())
        for j in range(n_tiles - min_used):
            pl.when(nused_ref[0] + j < n_tiles)(lambda j=j: fill(nused_ref[0] + j).start())
        for e in range(N_EXPERTS):
            pl.when(has_ref[e] > 0)(lambda e=e: fill(last_ref[e]).wait())
        for j in range(n_tiles - min_used):
            pl.when(nused_ref[0] + j < n_tiles)(lambda j=j: fill(nused_ref[0] + j).wait())

    stage[slot] = h_ref[...]
    for r in range(tt):
        for k in range(TOP_K):
            row = dest_ref[(i * tt + r) * TOP_K + k]
            pltpu.make_async_copy(stage.at[slot, pl.ds(r, 1), :], x_hbm.at[pl.ds(row, 1), :],
                                  sems.at[slot]).start(priority=k % 2)

    def wait_slot(s):
        for _ in range(TOP_K):
            pltpu.make_async_copy(stage.at[s], x_hbm.at[pl.ds(0, tt), :], sems.at[s]).wait()

    @pl.when(i > 0)
    def _():
        wait_slot(1 - slot)

    @pl.when(i == n - 1)
    def _():
        wait_slot(slot)


def _dispatch(h, dest, last_tile, has_rows, n_used, n_tiles):
    t = h.shape[0]
    tt = _pick_tile(t, (256, 128, 64, 16, 8))
    min_used = (t * TOP_K) // MOE_TILE
    grid_spec = pltpu.PrefetchScalarGridSpec(
        num_scalar_prefetch=4,
        grid=(t // tt,),
        in_specs=[pl.BlockSpec((tt, D_MODEL), lambda i, *_: (i, 0))],
        out_specs=pl.BlockSpec(memory_space=pl.ANY),
        scratch_shapes=[pltpu.VMEM((2, tt, D_MODEL), F32), pltpu.VMEM((MOE_TILE, D_MODEL), F32),
                        pltpu.SemaphoreType.DMA((2,)), pltpu.SemaphoreType.DMA],
    )
    return pl.pallas_call(
        functools.partial(_dispatch_kernel, tt=tt, n_tiles=n_tiles, min_used=min_used),
        grid_spec=grid_spec,
        out_shape=jax.ShapeDtypeStruct((n_tiles * MOE_TILE, D_MODEL), F32),
        compiler_params=_cparams(("arbitrary",)),
        name="moe_dispatch",
    )(dest, last_tile, has_rows, n_used, h)


def _expert_kernel(te_ref, nused_ref, x_ref, wgu_ref, bgu_ref, wd_ref, bd_ref, y_ref, wgu_bf, wd_bf):
    i = pl.program_id(0)
    used = i < nused_ref[0]
    prev = te_ref[jnp.maximum(i - 1, 0)]
    fresh = jnp.logical_or(i == 0, te_ref[i] != prev)

    @pl.when(jnp.logical_and(used, fresh))
    def _():
        wgu_bf[...] = wgu_ref[0].astype(BF16)
        wd_bf[...] = wd_ref[0].astype(BF16)

    @pl.when(used)
    def _():
        x = x_ref[...].astype(BF16)
        hcat = _dot(x, wgu_bf[...]) + bgu_ref[0]
        dff = hcat.shape[1] // 2
        gate = jnp.minimum(hcat[:, :dff], SWIGLU_LIMIT)
        up = jnp.clip(hcat[:, dff:], -SWIGLU_LIMIT, SWIGLU_LIMIT)
        act = (up + 1.0) * gate * _sigmoid(SWIGLU_ALPHA * gate)
        y_ref[...] = _dot(act.astype(BF16), wd_bf[...]) + bd_ref[0]

    @pl.when(jnp.logical_not(used))
    def _():
        y_ref[...] = jnp.zeros_like(y_ref)


def _experts(x_sorted, tile_expert, n_used, wgu, bgu, wd, bd):
    tm = MOE_TILE
    n_tiles = x_sorted.shape[0] // tm
    dff2 = wgu.shape[2]
    grid_spec = pltpu.PrefetchScalarGridSpec(
        num_scalar_prefetch=2,
        grid=(n_tiles,),
        in_specs=[pl.BlockSpec((tm, D_MODEL), lambda i, te, n: (jnp.minimum(i, n[0] - 1), 0)),
                  pl.BlockSpec((1, D_MODEL, dff2), lambda i, te, n: (te[i], 0, 0)),
                  pl.BlockSpec((1, 1, dff2), lambda i, te, n: (te[i], 0, 0)),
                  pl.BlockSpec((1, dff2 // 2, D_MODEL), lambda i, te, n: (te[i], 0, 0)),
                  pl.BlockSpec((1, 1, D_MODEL), lambda i, te, n: (te[i], 0, 0))],
        out_specs=pl.BlockSpec((tm, D_MODEL), lambda i, te, n: (i, 0)),
        scratch_shapes=[pltpu.VMEM((D_MODEL, dff2), BF16), pltpu.VMEM((dff2 // 2, D_MODEL), BF16)],
    )
    return pl.pallas_call(
        _expert_kernel,
        grid_spec=grid_spec,
        out_shape=jax.ShapeDtypeStruct((n_tiles * tm, D_MODEL), F32),
        compiler_params=_cparams(("arbitrary",)),
        name="moe_experts",
    )(tile_expert, n_used, x_sorted, wgu, bgu, wd, bd)


def _combine_kernel(dest_ref, y_hbm, gate_ref, h_ref, lng_ref, lnb_ref, o_ref, buf, sems, *, tt, alpha):
    i = pl.program_id(0)
    n = pl.num_programs(0)
    slot = i % 2

    def issue_tile(tile, s):
        for r in range(tt):
            for k in range(TOP_K):
                row = dest_ref[(tile * tt + r) * TOP_K + k]
                pltpu.make_async_copy(y_hbm.at[pl.ds(row, 1), :], buf.at[s, k, pl.ds(r, 1), :],
                                      sems.at[s]).start(priority=k % 2)

    @pl.when(i == 0)
    def _():
        issue_tile(0, 0)

    @pl.when(i + 1 < n)
    def _():
        issue_tile(i + 1, 1 - slot)

    for k in range(TOP_K):
        pltpu.make_async_copy(y_hbm.at[pl.ds(0, tt), :], buf.at[slot, k], sems.at[slot]).wait()
    gates = gate_ref[...]
    ffn = gates[:, 0:1] * buf[slot, 0]
    for k in range(1, TOP_K):
        ffn = ffn + gates[:, k:k + 1] * buf[slot, k]
    o_ref[...] = _layer_norm(alpha * h_ref[...] + ffn, lng_ref[...], lnb_ref[...])


def _combine(y_sorted, dest, gates, h, lng, lnb, alpha):
    t = h.shape[0]
    tt = _pick_tile(t, (128, 64, 16, 8))
    grid_spec = pltpu.PrefetchScalarGridSpec(
        num_scalar_prefetch=1,
        grid=(t // tt,),
        in_specs=[pl.BlockSpec(memory_space=pl.ANY),
                  pl.BlockSpec((tt, TOP_K), lambda i, d: (i, 0)),
                  pl.BlockSpec((tt, D_MODEL), lambda i, d: (i, 0)),
                  pl.BlockSpec((1, D_MODEL), lambda i, d: (0, 0)),
                  pl.BlockSpec((1, D_MODEL), lambda i, d: (0, 0))],
        out_specs=pl.BlockSpec((tt, D_MODEL), lambda i, d: (i, 0)),
        scratch_shapes=[pltpu.VMEM((2, TOP_K, tt, D_MODEL), F32), pltpu.SemaphoreType.DMA((2,))],
    )
    return pl.pallas_call(
        functools.partial(_combine_kernel, tt=tt, alpha=alpha),
        grid_spec=grid_spec,
        out_shape=jax.ShapeDtypeStruct((t, D_MODEL), F32),
        compiler_params=_cparams(("arbitrary",)),
        name="moe_combine",
    )(dest, y_sorted, gates, h, lng, lnb)


def _moe_plan(eidx, rank, counts, n_tiles, expert0):
    tm = MOE_TILE
    cnt = counts[0, :N_EXPERTS]
    padded = (cnt + tm - 1) // tm * tm
    ends = jnp.cumsum(padded)
    starts = ends - padded
    dest = (starts[eidx] + rank).reshape(-1).astype(jnp.int32)
    tile_start = jnp.arange(n_tiles, dtype=jnp.int32) * tm
    tile_expert = jnp.sum((tile_start[:, None] >= ends[None, :]).astype(jnp.int32), axis=1)
    tile_expert = (jnp.minimum(tile_expert, N_EXPERTS - 1) + expert0).astype(jnp.int32)
    n_used = (ends[-1] // tm).astype(jnp.int32).reshape(1)
    last_tile = (ends // tm - 1).astype(jnp.int32)
    has_rows = (padded > 0).astype(jnp.int32)
    return dest, tile_expert, n_used, last_tile, has_rows


def _layer(x2d, lw, shared, layer, depth, dims, alpha, kv_prev):
    nb_p, seq_p, nb_s, seq_s = dims
    t_p = nb_p * seq_p
    t_s = nb_s * seq_s
    page_table = shared["page_table"]
    page0 = layer * shared["n_pool"]
    proj = _project(x2d, lw["wt_main"])
    kbuf, vbuf = _kv_prompt(x2d, lw["wt_kv"], layer, depth, nb_p, seq_p, kv_prev)
    kv_s = _project(x2d, lw["wt_kv"], row0=t_p, rows=t_s)
    logft_p, cum_p, cumt_p = _fox_gate_prompt(x2d, lw["wt_f"], lw["b_f"], nb_p, seq_p)
    oa_p, s_p = _hgrn(proj, lw["lb"], lw["hgrn_g"], None, 0, 0, nb_p, seq_p)
    oa_s, s_s = _hgrn(proj, lw["lb"], lw["hgrn_g"], shared["state"], layer * nb_s, t_p, nb_s, seq_s)
    ob_p = _fox_prompt(proj, kbuf, vbuf, layer, cum_p, cumt_p, nb_p, seq_p)
    ob_s, logft_s = _fox_sample(proj, kv_s, x2d, lw["wt_f"], lw["b_f"], shared["cache_kt"], shared["cache_vt"],
                                shared["cache_logf_t"], page_table, page0, t_p, nb_s, seq_s)
    h, eidx, gates, rank, counts = _mix(oa_p, oa_s, ob_p, ob_s, proj, x2d, lw["w_a"], lw["w_b"], lw["w_o"],
                                        lw["ln_mix_g"], lw["ln_mix_b"], lw["wr_hi"], lw["wr_lo"], lw["b_r"], alpha)
    t = x2d.shape[0]
    n_tiles = (t * TOP_K + N_EXPERTS * (MOE_TILE - 1)) // MOE_TILE + 1
    dest, tile_expert, n_used, last_tile, has_rows = _moe_plan(eidx, rank, counts, n_tiles, layer * N_EXPERTS)
    x_sorted = _dispatch(h, dest, last_tile, has_rows, n_used, n_tiles)
    y_sorted = _experts(x_sorted, tile_expert, n_used,
                        shared["w_gate_up"], shared["b_gate_up"], shared["w_down"], shared["b_down"])
    y = _combine(y_sorted, dest, gates, h, lw["ln_ffn_g"], lw["ln_ffn_b"], alpha)
    return y, (kbuf, vbuf), kv_s, logft_p, logft_s, s_p, s_s


def kernel(x_prompt, x_sample, cache_k, cache_v, cache_logf, state_hgrn, page_table, w_in, b_fox_f, lb_logits,
           hgrn_norm_g, w_branch_a, w_branch_b, w_out, ln_mix_g, ln_mix_b, w_router, b_router, w_gate_up,
           b_gate_up, w_down, b_down, ln_ffn_g, ln_ffn_b):
    depth = w_in.shape[0]
    nb_p, seq_p, d = x_prompt.shape
    nb_s, seq_s, _ = x_sample.shape
    t_p, t_s = nb_p * seq_p, nb_s * seq_s
    alpha = (2 * depth) ** 0.25
    n_pool, page = cache_k.shape[1], cache_k.shape[2]
    dff2 = w_gate_up.shape[-1]

    pl_soft = jax.nn.softmax(lb_logits.astype(F32), axis=0)
    lower_bounds = jnp.cumsum(pl_soft, axis=0) - pl_soft[0:1]

    x2d = jnp.concatenate([x_prompt.reshape(t_p, d), x_sample.reshape(t_s, d)], axis=0)
    shared = {
        "page_table": page_table, "n_pool": n_pool,
        "cache_kt": cache_k.transpose(0, 1, 3, 4, 2).reshape(depth * n_pool, D_MODEL, page),
        "cache_vt": cache_v.transpose(0, 1, 3, 4, 2).reshape(depth * n_pool, D_MODEL, page),
        "cache_logf_t": cache_logf.transpose(0, 1, 3, 2).reshape(depth * n_pool, B_HEADS, page),
        "state": state_hgrn.reshape(depth * nb_s, A_HEADS, A_DK, A_DK),
        "w_gate_up": w_gate_up.reshape(depth * N_EXPERTS, d, dff2),
        "b_gate_up": b_gate_up.reshape(depth * N_EXPERTS, 1, dff2),
        "w_down": w_down.reshape(depth * N_EXPERTS, dff2 // 2, d),
        "b_down": b_down.reshape(depth * N_EXPERTS, 1, d),
    }
    wt_in = w_in.transpose(0, 2, 1)
    c_k = 5 * D_MODEL
    c_f = 7 * D_MODEL
    pad_r = LANES - N_EXPERTS
    kv_prev = None
    fp, sp, ks, vs, fs, ss = [], [], [], [], [], []
    for l in range(depth):
        wt = wt_in[l]
        wr_t = jnp.pad(w_router[l].T, ((0, pad_r), (0, 0)))
        wr_hi = wr_t.astype(BF16)
        lw = {
            "wt_main": jnp.concatenate([wt[:c_k], wt[c_f + B_HEADS:]], axis=0).astype(BF16),
            "wt_kv": wt[c_k:c_f].astype(BF16),
            "wt_f": wt[c_f:c_f + B_HEADS].astype(BF16),
            "b_f": b_fox_f[l],
            "lb": lower_bounds[l].reshape(1, d),
            "hgrn_g": hgrn_norm_g[l].reshape(1, A_DK),
            "w_a": w_branch_a[l].astype(BF16), "w_b": w_branch_b[l].astype(BF16), "w_o": w_out[l].astype(BF16),
            "ln_mix_g": ln_mix_g[l].reshape(1, d), "ln_mix_b": ln_mix_b[l].reshape(1, d),
            "ln_ffn_g": ln_ffn_g[l].reshape(1, d), "ln_ffn_b": ln_ffn_b[l].reshape(1, d),
            "b_r": jnp.pad(b_router[l], (0, pad_r), constant_values=MASK_VALUE).reshape(1, LANES),
            "wr_hi": wr_hi, "wr_lo": (wr_t - wr_hi.astype(F32)).astype(BF16),
        }
        x2d, kv_prev, kv_s, logft_p, logft_s, s_p, s_s = _layer(
            x2d, lw, shared, l, depth, (nb_p, seq_p, nb_s, seq_s), alpha, kv_prev)
        ks.append(kv_s[:, :D_MODEL].reshape(nb_s, seq_s, B_HEADS, B_DH))
        vs.append(kv_s[:, D_MODEL:].reshape(nb_s, seq_s, B_HEADS, B_DH))
        fp.append(logft_p)
        fs.append(logft_s)
        sp.append(s_p)
        ss.append(s_s)
    y_p = x2d[:t_p].reshape(nb_p, seq_p, d)
    y_s = x2d[t_p:].reshape(nb_s, seq_s, d)
    kbuf, vbuf = kv_prev
    k_p = kbuf.reshape(depth, nb_p, B_HEADS, B_DH, seq_p).transpose(0, 1, 4, 2, 3)
    v_p = vbuf.reshape(depth, nb_p, B_HEADS, B_DH, seq_p).transpose(0, 1, 4, 2, 3)
    f_p = jnp.stack(fp).transpose(0, 1, 3, 2)
    f_s = jnp.stack(fs).transpose(0, 1, 3, 2)
    return (y_p, y_s, k_p, v_p, f_p, jnp.stack(sp), jnp.stack(ks), jnp.stack(vs), f_s, jnp.stack(ss))
```

```python
import functools

import jax
import jax.numpy as jnp
from jax import lax
from jax.experimental import pallas as pl
from jax.experimental.pallas import tpu as pltpu

F32 = jnp.float32
BF16 = jnp.bfloat16

D_MODEL = 1024
A_HEADS = 8
A_DK = 128
B_HEADS = 16
B_DH = 64
N_EXPERTS = 32
TOP_K = 4
SWIGLU_LIMIT = 7.0
SWIGLU_ALPHA = 1.702
LN_EPS = 1e-5
RMS_EPS = 1e-6
MASK_VALUE = -1e30
LOG2E = 1.4426950408889634
LANES = 128
HGRN_CHUNK = 64
HGRN_SAFE_DECAY = 80.0
VMEM_LIMIT = 56 * 1024 * 1024
PROJ_Q0 = 4 * D_MODEL
PROJ_G0 = 5 * D_MODEL
MOE_TILE = 512


def _pick_tile(n, candidates):
    for c in candidates:
        if n % c == 0:
            return c
    return n


def _cparams(sem, vmem=None):
    return pltpu.CompilerParams(dimension_semantics=sem, vmem_limit_bytes=vmem or VMEM_LIMIT)


def _split3(x):
    hi = x.astype(BF16)
    r = x - hi.astype(F32)
    mid = r.astype(BF16)
    lo = (r - mid.astype(F32)).astype(BF16)
    return hi, mid, lo


def _dot(a, b):
    return jnp.dot(a, b, preferred_element_type=F32)


def _dot_nt(a, b):
    return lax.dot_general(a, b, (((1,), (1,)), ((), ())), preferred_element_type=F32)


def _dot_tn(a, b):
    return lax.dot_general(a, b, (((0,), (0,)), ((), ())), preferred_element_type=F32)


def _tri(n, upper=False):
    r = lax.broadcasted_iota(jnp.int32, (n, n), 0)
    c = lax.broadcasted_iota(jnp.int32, (n, n), 1)
    return ((r <= c) if upper else (c <= r)).astype(BF16)


def _cumsum_rows(x):
    tri = _tri(x.shape[0])
    hi, mid, lo = _split3(x)
    return _dot(tri, hi) + _dot(tri, mid) + _dot(tri, lo)


def _cumsum_lanes(x):
    tri = _tri(x.shape[1], upper=True)
    hi, mid, lo = _split3(x)
    return _dot(hi, tri) + _dot(mid, tri) + _dot(lo, tri)


def _log_sigmoid(z):
    return jnp.minimum(z, 0.0) - jnp.log(1.0 + jnp.exp(-jnp.abs(z)))


def _sigmoid(z):
    return 1.0 / (1.0 + jnp.exp(-z))


def _silu(z):
    return z * _sigmoid(z)


def _layer_norm(x, g, b):
    mu = jnp.mean(x, axis=-1, keepdims=True)
    xc = x - mu
    var = jnp.mean(xc * xc, axis=-1, keepdims=True)
    return xc * lax.rsqrt(var + LN_EPS) * g + b


def _drop_ref(kern, idx):
    def wrapped(*refs):
        return kern(*(refs[:idx] + refs[idx + 1:]))
    return wrapped


def _proj_kernel(x_ref, wt_ref, o_ref):
    o_ref[...] = _dot_nt(x_ref[...].astype(BF16), wt_ref[...])


def _project(x2d, wt, row0=0, rows=None):
    d = x2d.shape[1]
    t = x2d.shape[0] if rows is None else rows
    n = wt.shape[0]
    tm = _pick_tile(t, (1280, 640, 512, 256))
    tn = _pick_tile(n, (1024, 512, 256, 128))
    rb0 = row0 // tm
    return pl.pallas_call(
        _proj_kernel,
        grid=(t // tm, n // tn),
        in_specs=[pl.BlockSpec((tm, d), lambda i, j: (rb0 + i, 0)),
                  pl.BlockSpec((tn, d), lambda i, j: (j, 0))],
        out_specs=pl.BlockSpec((tm, tn), lambda i, j: (i, j)),
        out_shape=jax.ShapeDtypeStruct((t, n), F32),
        compiler_params=_cparams(("parallel", "parallel")),
        name="in_proj",
    )(x2d, wt)


def _kv_prompt_kernel(x_ref, wt_ref, k_ref, v_ref, *, layer, fill_others):
    def compute():
        kv = _dot_nt(wt_ref[...], x_ref[...].astype(BF16))
        k_ref[0, 0] = kv[:D_MODEL]
        v_ref[0, 0] = kv[D_MODEL:]

    if not fill_others:
        compute()
        return
    s = pl.program_id(0)
    pl.when(s == layer)(compute)

    @pl.when(s != layer)
    def _():
        k_ref[...] = jnp.zeros_like(k_ref)
        v_ref[...] = jnp.zeros_like(v_ref)


def _kv_prompt(x2d, wt_kv, layer, depth, nb, seq, prev):
    d = x2d.shape[1]
    tl = _pick_tile(seq, (512, 256, 128))
    nl = seq // tl
    first = prev is None
    in_specs = [pl.BlockSpec((tl, d), lambda s, b, i: (b * nl + i, 0)),
                pl.BlockSpec((2 * D_MODEL, d), lambda s, b, i: (0, 0))]
    args = [x2d, wt_kv]
    kern = functools.partial(_kv_prompt_kernel, layer=layer, fill_others=first)
    aliases = {}
    if first:
        ospec = pl.BlockSpec((1, 1, D_MODEL, tl), lambda s, b, i: (s, b, 0, i))
    else:
        in_specs += [pl.BlockSpec(memory_space=pl.ANY)] * 2
        args += list(prev)
        aliases = {2: 0, 3: 1}
        kern = _drop_ref(_drop_ref(kern, 2), 3)
        ospec = pl.BlockSpec((1, 1, D_MODEL, tl), lambda s, b, i: (layer, b, 0, i))
    out = jax.ShapeDtypeStruct((depth, nb, D_MODEL, seq), F32)
    return pl.pallas_call(
        kern,
        grid=(depth if first else 1, nb, nl),
        in_specs=in_specs,
        out_specs=[ospec, ospec],
        out_shape=[out, out],
        input_output_aliases=aliases,
        compiler_params=_cparams(("parallel", "parallel", "parallel")),
        name="kv_prompt",
    )(*args)


def _fox_gate_prompt_kernel(x_ref, wft_ref, bf_ref, bft_ref, logft_ref, cum_ref, cumt_ref, carry_ref, carryt_ref):
    i = pl.program_id(1)

    @pl.when(i == 0)
    def _():
        carry_ref[...] = jnp.zeros_like(carry_ref)
        carryt_ref[...] = jnp.zeros_like(carryt_ref)

    xb = x_ref[...].astype(BF16)
    logf = _log_sigmoid(_dot_nt(xb, wft_ref[...]) + bf_ref[...])
    logft = _log_sigmoid(_dot_nt(wft_ref[...], xb) + bft_ref[...])
    tl = logf.shape[0]
    cs = _cumsum_rows(logf) + carry_ref[...]
    cst = _cumsum_lanes(logft) + carryt_ref[...]
    carry_ref[...] = cs[tl - 1:tl, :]
    carryt_ref[...] = cst[:, tl - 1:tl]
    logft_ref[0] = logft
    cum_ref[0] = cs
    cumt_ref[0] = cst


def _fox_gate_prompt(x2d, wft, bf, nb, seq):
    d = x2d.shape[1]
    tl = _pick_tile(seq, (256, 128))
    nl = seq // tl
    out_t = jax.ShapeDtypeStruct((nb, B_HEADS, seq), F32)
    tspec = pl.BlockSpec((1, B_HEADS, tl), lambda b, i: (b, 0, i))
    return pl.pallas_call(
        _fox_gate_prompt_kernel,
        grid=(nb, nl),
        in_specs=[pl.BlockSpec((tl, d), lambda b, i: (b * nl + i, 0)),
                  pl.BlockSpec((B_HEADS, d), lambda b, i: (0, 0)),
                  pl.BlockSpec((1, B_HEADS), lambda b, i: (0, 0)),
                  pl.BlockSpec((B_HEADS, 1), lambda b, i: (0, 0))],
        out_specs=[tspec, pl.BlockSpec((1, tl, B_HEADS), lambda b, i: (b, i, 0)), tspec],
        out_shape=[out_t, jax.ShapeDtypeStruct((nb, seq, B_HEADS), F32), out_t],
        scratch_shapes=[pltpu.VMEM((1, B_HEADS), F32), pltpu.VMEM((B_HEADS, 1), F32)],
        compiler_params=_cparams(("parallel", "arbitrary")),
        name="fox_gate_prompt",
    )(x2d, wft, bf.reshape(1, B_HEADS), bf.reshape(B_HEADS, 1))


def _hgrn_kernel(*refs, chunk, n_chunks, has_s0):
    if has_s0:
        (aq_ref, af_ref, ai_ref, ag_ref, lb_ref, g_ref, s0_ref, o_ref, snew_ref,
         st_ref, a_ref, gs_ref, ks_ref, qt_ref, kt_ref, vb_ref) = refs
    else:
        (aq_ref, af_ref, ai_ref, ag_ref, lb_ref, g_ref, o_ref, snew_ref,
         st_ref, a_ref, gs_ref, ks_ref, qt_ref, kt_ref, vb_ref) = refs
        s0_ref = None
    i = pl.program_id(1)
    c = chunk
    rows = c * n_chunks

    @pl.when(i == 0)
    def _():
        for h in range(A_HEADS):
            if has_s0:
                st_ref[h] = s0_ref[0, h].T
            else:
                st_ref[h] = jnp.zeros((A_DK, A_DK), F32)

    lbv = lb_ref[...]
    gv = g_ref[...]
    scale = A_DK ** -0.5

    z = af_ref[...]
    logf = jnp.log(lbv + (1.0 - lbv) * _sigmoid(z))
    kk = (1.0 - lbv) * _sigmoid(-z)
    rr = lax.broadcasted_iota(jnp.int32, (rows, rows), 0)
    cc = lax.broadcasted_iota(jnp.int32, (rows, rows), 1)
    tri = ((rr // c == cc // c) & (cc <= rr)).astype(BF16)
    hi, mid, lo = _split3(logf)
    g_cum = _dot(tri, hi) + _dot(tri, mid) + _dot(tri, lo)
    gs_ref[...] = g_cum
    ks_ref[...] = kk
    qt_ref[...] = (_silu(aq_ref[...]) * scale * jnp.exp(g_cum)).astype(BF16)
    kt_ref[...] = (kk * jnp.exp(-g_cum)).astype(BF16)
    vb_ref[...] = ai_ref[...].astype(BF16)
    a_ref[...] = jnp.zeros_like(a_ref)

    row = lax.broadcasted_iota(jnp.int32, (c, c), 0)
    col = lax.broadcasted_iota(jnp.int32, (c, c), 1)
    causal = col <= row

    safes = []
    for ci in range(n_chunks):
        r = slice(ci * c, (ci + 1) * c)
        safe = jnp.min(gs_ref[(ci + 1) * c - 1:(ci + 1) * c, :]) >= -HGRN_SAFE_DECAY
        safes.append(safe)

        @pl.when(jnp.logical_not(safe))
        def _():
            q = _silu(aq_ref[r, :]) * scale
            g_c = gs_ref[r, :]

            def col_j(j, carry):
                gj = gs_ref[pl.ds(ci * c + j, 1), :]
                kj = ks_ref[pl.ds(ci * c + j, 1), :]
                w = q * kj * jnp.exp(jnp.minimum(g_c - gj, 0.0))
                for h in range(A_HEADS):
                    hs = slice(h * A_DK, (h + 1) * A_DK)
                    cj = jnp.sum(w[:, hs], axis=1, keepdims=True)
                    a_ref[ci, h] = a_ref[ci, h] + jnp.where(col == j, cj, 0.0)
                return carry

            lax.fori_loop(0, c, col_j, 0)

    for ci in range(n_chunks):
        r = slice(ci * c, (ci + 1) * c)
        safe = safes[ci]
        g_c = gs_ref[r, :]
        gend = g_c[c - 1:c, :]
        khat = (ks_ref[r, :] * jnp.exp(gend - g_c)).astype(BF16)
        egend = jnp.exp(gend)
        for h in range(A_HEADS):
            hs = slice(h * A_DK, (h + 1) * A_DK)
            qt = qt_ref[r, hs]
            vb = vb_ref[r, hs]
            a_fast = _dot_nt(qt, kt_ref[r, hs])
            att = jnp.where(causal, jnp.where(safe, a_fast, a_ref[ci, h]), 0.0).astype(BF16)
            st = st_ref[h]
            o = _dot(att, vb) + _dot_nt(qt, st.astype(BF16))
            st_ref[h] = st * egend[:, hs] + _dot_tn(vb, khat[:, hs])
            ms = jnp.mean(o * o, axis=-1, keepdims=True)
            o = o * lax.rsqrt(ms + RMS_EPS) * gv
            o_ref[r, hs] = o * _silu(ag_ref[r, hs])

    @pl.when(i == pl.num_programs(1) - 1)
    def _():
        for h in range(A_HEADS):
            snew_ref[0, h] = st_ref[h].T


def _hgrn(proj, lb, g, s0, s0_base, row0, nb, seq):
    chunk = min(HGRN_CHUNK, seq)
    lb_rows = _pick_tile(seq, (256, chunk))
    nl = seq // lb_rows
    rb0 = row0 // lb_rows
    has_s0 = s0 is not None
    in_specs = [pl.BlockSpec((lb_rows, D_MODEL), functools.partial(lambda b, i, k: (rb0 + b * nl + i, k), k=k))
                for k in range(4)]
    in_specs += [pl.BlockSpec((1, D_MODEL), lambda b, i: (0, 0)),
                 pl.BlockSpec((1, A_DK), lambda b, i: (0, 0))]
    args = [proj, proj, proj, proj, lb, g]
    if has_s0:
        in_specs.append(pl.BlockSpec((1, A_HEADS, A_DK, A_DK), lambda b, i: (s0_base + b, 0, 0, 0)))
        args.append(s0)
    kern = functools.partial(_hgrn_kernel, chunk=chunk, n_chunks=lb_rows // chunk, has_s0=has_s0)
    return pl.pallas_call(
        kern,
        grid=(nb, nl),
        in_specs=in_specs,
        out_specs=[pl.BlockSpec((lb_rows, D_MODEL), lambda b, i: (b * nl + i, 0)),
                   pl.BlockSpec((1, A_HEADS, A_DK, A_DK), lambda b, i: (b, 0, 0, 0))],
        out_shape=[jax.ShapeDtypeStruct((nb * seq, D_MODEL), F32),
                   jax.ShapeDtypeStruct((nb, A_HEADS, A_DK, A_DK), F32)],
        scratch_shapes=[pltpu.VMEM((A_HEADS, A_DK, A_DK), F32),
                        pltpu.VMEM((lb_rows // chunk, A_HEADS, chunk, chunk), F32),
                        pltpu.VMEM((lb_rows, D_MODEL), F32), pltpu.VMEM((lb_rows, D_MODEL), F32),
                        pltpu.VMEM((lb_rows, D_MODEL), BF16), pltpu.VMEM((lb_rows, D_MODEL), BF16),
                        pltpu.VMEM((lb_rows, D_MODEL), BF16)],
        compiler_params=_cparams(("parallel", "arbitrary")),
        name="hgrn",
    )(*args)


def _fox_prompt_kernel(q_ref, k_ref, v_ref, cum_ref, cumt_ref, o_ref, kn_ref, qt_ref, vt_ref, s_ref, p_ref,
                       *, tq, tk):
    j = pl.program_id(1)
    seq = q_ref.shape[0]
    nq = seq // tq
    hpair = (2 * j, 2 * j + 1)

    kn_ref[:, :LANES] = k_ref[0, 0].T.astype(BF16)
    h16 = lax.broadcasted_iota(jnp.int32, (B_HEADS, LANES), 0)
    l16 = lax.broadcasted_iota(jnp.int32, (B_HEADS, LANES), 1)
    kaug = jnp.zeros((seq, LANES), F32)
    for part, term in enumerate(_split3(cum_ref[0] * LOG2E)):
        sel = jnp.where(((h16 == hpair[0]) & (l16 == part)) | ((h16 == hpair[1]) & (l16 == 3 + part)), -1.0, 0.0)
        kaug = kaug + _dot(term, sel.astype(BF16))
    lane_s = lax.broadcasted_iota(jnp.int32, (seq, LANES), 1)
    kn_ref[:, LANES:] = (kaug + jnp.where((lane_s >= 6) & (lane_s < 9), 1.0, 0.0)).astype(BF16)

    qt = (q_ref[...] * (B_DH ** -0.5 * LOG2E)).T
    row = lax.broadcasted_iota(jnp.int32, (LANES, seq), 0)
    r16 = lax.broadcasted_iota(jnp.int32, (LANES, B_HEADS), 0)
    c16 = lax.broadcasted_iota(jnp.int32, (LANES, B_HEADS), 1)
    cumt_terms = _split3(cumt_ref[0] * LOG2E)
    for h in range(2):
        qt_ref[h, :LANES, :] = jnp.where(row // B_DH == h, qt, 0.0).astype(BF16)
        qaug = jnp.where((row >= 3 * h) & (row < 3 * h + 3), 1.0, 0.0)
        for part, term in enumerate(cumt_terms):
            sel = jnp.where((r16 == 6 + part) & (c16 == hpair[h]), 1.0, 0.0)
            qaug = qaug + _dot(sel.astype(BF16), term)
        qt_ref[h, LANES:, :] = qaug.astype(BF16)
    vt_ref[...] = v_ref[0, 0].astype(BF16)

    kpos = lax.broadcasted_iota(jnp.int32, (tk, tq), 0)
    qpos = lax.broadcasted_iota(jnp.int32, (tk, tq), 1)
    kpq = tq // tk

    sub = tk // 8

    def fold(x):
        return x.reshape(sub, 8, tq)

    for qi in range(nq):
        q0 = qi * tq
        nfull = qi * kpq
        outs = []
        for h in range(2):
            qa = qt_ref[h, :, q0:q0 + tq]

            def scores(ki, mx, qa=qa, h=h):
                ks = pl.ds(pl.multiple_of(ki * tk, tk), tk)
                s = _dot(kn_ref[ks, :], qa)
                s_ref[h, ks, :] = s
                return jnp.maximum(mx, jnp.max(fold(s), axis=0))

            mx = jnp.full((8, tq), -jnp.inf, F32)
            if nfull:
                mx = lax.fori_loop(0, nfull, scores, mx, unroll=2)
            for dk in range(kpq):
                k0 = (nfull + dk) * tk
                s = _dot(kn_ref[k0:k0 + tk, :], qa)
                s = jnp.where(k0 + kpos <= q0 + qpos, s, MASK_VALUE)
                s_ref[h, k0:k0 + tk, :] = s
                mx = jnp.maximum(mx, jnp.max(fold(s), axis=0))
            m = jnp.max(mx, axis=0, keepdims=True)

            def probs(ki, lsum, m=m, h=h):
                ks = pl.ds(pl.multiple_of(ki * tk, tk), tk)
                p = jnp.exp2(s_ref[h, ks, :] - m)
                p_ref[h, ks, :] = p.astype(BF16)
                return lsum + jnp.sum(fold(p), axis=0)

            nk = nfull + kpq
            lsum = lax.fori_loop(0, nk, probs, jnp.zeros((8, tq), F32), unroll=2)
            l = jnp.sum(lsum, axis=0, keepdims=True)
            acc = _dot(vt_ref[h * B_DH:(h + 1) * B_DH, 0:nk * tk], p_ref[h, 0:nk * tk, :])
            outs.append(acc * (1.0 / l))
        o_ref[q0:q0 + tq, :] = jnp.concatenate(outs, axis=0).T


def _fox_prompt(proj, kbuf, vbuf, layer, cum, cum_t, nb, seq):
    t = nb * seq
    tq = _pick_tile(seq, (512, 256, 128))
    tk = min(tq, 256)
    npair = B_HEADS // 2
    c0 = PROJ_Q0 // LANES
    kvspec = pl.BlockSpec((1, 1, LANES, seq), lambda b, j: (layer, b, j, 0))
    return pl.pallas_call(
        functools.partial(_fox_prompt_kernel, tq=tq, tk=tk),
        grid=(nb, npair),
        in_specs=[pl.BlockSpec((seq, LANES), lambda b, j: (b, c0 + j)), kvspec, kvspec,
                  pl.BlockSpec((1, seq, B_HEADS), lambda b, j: (b, 0, 0)),
                  pl.BlockSpec((1, B_HEADS, seq), lambda b, j: (b, 0, 0))],
        out_specs=pl.BlockSpec((seq, LANES), lambda b, j: (b, j)),
        out_shape=jax.ShapeDtypeStruct((t, D_MODEL), F32),
        scratch_shapes=[pltpu.VMEM((seq, 2 * LANES), BF16), pltpu.VMEM((2, 2 * LANES, seq), BF16),
                        pltpu.VMEM((LANES, seq), BF16), pltpu.VMEM((2, seq, tq), F32),
                        pltpu.VMEM((2, seq, tq), BF16)],
        compiler_params=_cparams(("parallel", "parallel")),
        name="fox_prompt",
    )(proj, kbuf, vbuf, cum, cum_t)


def _fox_sample_kernel(*refs, pages_per_step, ns):
    pps = pages_per_step
    q_ref, kvn_ref, x_ref, wft_ref, bft_ref = refs[1:6]
    k_refs = refs[6:6 + pps]
    v_refs = refs[6 + pps:6 + 2 * pps]
    f_refs = refs[6 + 2 * pps:6 + 3 * pps]
    o_ref, logft_ref = refs[6 + 3 * pps:8 + 3 * pps]
    qbd_ref, fq_ref, cn_ref, past_ref, m_ref, l_ref, acc_ref = refs[8 + 3 * pps:]
    g = pl.program_id(1)
    nrow = B_HEADS * ns
    rh = lax.broadcasted_iota(jnp.int32, (nrow, D_MODEL), 0) // ns
    ch = lax.broadcasted_iota(jnp.int32, (nrow, D_MODEL), 1) // B_DH

    def expand_rows(x):
        n = x.shape[1]
        return jnp.broadcast_to(x[:, None, :], (B_HEADS, ns, n)).reshape(nrow, n)

    @pl.when(g == 0)
    def _():
        q = q_ref[...] * (B_DH ** -0.5)
        qrep = jnp.broadcast_to(q[None], (B_HEADS, ns, D_MODEL)).reshape(nrow, D_MODEL)
        qbd_ref[...] = jnp.where(rh == ch, qrep, 0.0).astype(BF16)
        logft = _log_sigmoid(_dot_nt(wft_ref[...], x_ref[...].astype(BF16)) + bft_ref[...])
        logft_ref[0] = logft
        cn = _cumsum_lanes(logft)
        cn_ref[...] = cn
        tsel = (lax.broadcasted_iota(jnp.int32, (nrow, ns), 0) % ns
                == lax.broadcasted_iota(jnp.int32, (nrow, ns), 1))
        fq_ref[...] = jnp.sum(jnp.where(tsel, expand_rows(cn), 0.0), axis=1, keepdims=True)
        past_ref[...] = jnp.zeros_like(past_ref)
        m_ref[...] = jnp.full_like(m_ref, -jnp.inf)
        l_ref[...] = jnp.zeros_like(l_ref)
        acc_ref[...] = jnp.zeros_like(acc_ref)

    def update(s, pv):
        m_old = m_ref[...]
        m_new = jnp.maximum(m_old, jnp.max(s, axis=1, keepdims=True))
        alpha = jnp.exp(m_old - m_new)
        p = jnp.exp(s - m_new)
        l_ref[...] = alpha * l_ref[...] + jnp.sum(p, axis=1, keepdims=True)
        acc_ref[...] = alpha * acc_ref[...] + pv(p.astype(BF16))
        m_ref[...] = m_new

    page = k_refs[0].shape[2]
    lf = jnp.concatenate([f_refs[pi][0] for pi in range(pps)], axis=0)
    cs = _cumsum_lanes(lf)
    offset = past_ref[...]
    s_parts = []
    for pi in range(pps):
        cs_p = cs[pi * B_HEADS:(pi + 1) * B_HEADS, :] + offset
        offset = cs_p[:, page - 1:]
        kt = k_refs[pi][0].astype(BF16)
        s_parts.append(_dot(qbd_ref[...], kt) + fq_ref[...] - expand_rows(cs_p))
    past_ref[...] = offset

    def pv_pages(p):
        out = None
        for pi in range(pps):
            term = _dot_nt(p[:, pi * page:(pi + 1) * page], v_refs[pi][0].astype(BF16))
            out = term if out is None else out + term
        return out

    update(jnp.concatenate(s_parts, axis=1), pv_pages)

    @pl.when(g == pl.num_programs(1) - 1)
    def _():
        kb = kvn_ref[:, :D_MODEL].astype(BF16)
        vb = kvn_ref[:, D_MODEL:].astype(BF16)
        fk = expand_rows(cn_ref[...] + past_ref[...])
        s = _dot_nt(qbd_ref[...], kb) + fq_ref[...] - fk
        tq = lax.broadcasted_iota(jnp.int32, (nrow, ns), 0) % ns
        tk = lax.broadcasted_iota(jnp.int32, (nrow, ns), 1)
        s = jnp.where(tq >= tk, s, MASK_VALUE)
        update(s, lambda p: _dot(p, vb))
        o = acc_ref[...] / l_ref[...]
        for h in range(B_HEADS):
            o_ref[:, h * B_DH:(h + 1) * B_DH] = o[h * ns:(h + 1) * ns, h * B_DH:(h + 1) * B_DH]


def _fox_sample(proj, kv_new, x2d, wft, bf, cache_kt, cache_vt, cache_logf_t, page_table, page0, row0, nb, ns):
    d = x2d.shape[1]
    n_pages = page_table.shape[1]
    page = cache_kt.shape[2]
    pps = _pick_tile(n_pages, (8, 4, 2, 1))
    ng = n_pages // pps
    rb0 = row0 // ns
    nrow = B_HEADS * ns

    def page_spec(pi, rows):
        return pl.BlockSpec((1, rows, page), lambda b, g, pt: (page0 + pt[b, g * pps + pi], 0, 0))

    in_specs = [pl.BlockSpec((ns, D_MODEL), lambda b, g, pt: (rb0 + b, PROJ_Q0 // D_MODEL)),
                pl.BlockSpec((ns, 2 * D_MODEL), lambda b, g, pt: (b, 0)),
                pl.BlockSpec((ns, d), lambda b, g, pt: (rb0 + b, 0)),
                pl.BlockSpec((B_HEADS, d), lambda b, g, pt: (0, 0)),
                pl.BlockSpec((B_HEADS, 1), lambda b, g, pt: (0, 0))]
    in_specs += [page_spec(pi, D_MODEL) for pi in range(pps)] * 2
    in_specs += [page_spec(pi, B_HEADS) for pi in range(pps)]
    args = [proj, kv_new, x2d, wft, bf.reshape(B_HEADS, 1)]
    args += [cache_kt] * pps + [cache_vt] * pps + [cache_logf_t] * pps
    grid_spec = pltpu.PrefetchScalarGridSpec(
        num_scalar_prefetch=1,
        grid=(nb, ng),
        in_specs=in_specs,
        out_specs=[pl.BlockSpec((ns, D_MODEL), lambda b, g, pt: (b, 0)),
                   pl.BlockSpec((1, B_HEADS, ns), lambda b, g, pt: (b, 0, 0))],
        scratch_shapes=[pltpu.VMEM((nrow, D_MODEL), BF16), pltpu.VMEM((nrow, 1), F32),
                        pltpu.VMEM((B_HEADS, ns), F32), pltpu.VMEM((B_HEADS, 1), F32),
                        pltpu.VMEM((nrow, 1), F32), pltpu.VMEM((nrow, 1), F32),
                        pltpu.VMEM((nrow, D_MODEL), F32)],
    )
    return pl.pallas_call(
        functools.partial(_fox_sample_kernel, pages_per_step=pps, ns=ns),
        grid_spec=grid_spec,
        out_shape=[jax.ShapeDtypeStruct((nb * ns, D_MODEL), F32),
                   jax.ShapeDtypeStruct((nb, B_HEADS, ns), F32)],
        compiler_params=_cparams(("parallel", "arbitrary")),
        name="fox_sample",
    )(page_table, *args)


def _mix_kernel(oap_ref, oas_ref, obp_ref, obs_ref, ga_ref, gb_ref, x_ref, wa_ref, wb_ref, wo_ref, lng_ref,
                lnb_ref, wrh_ref, wrl_ref, br_ref, h_ref, eidx_ref, gate_ref, rank_ref, cnt_ref, carry_ref,
                *, alpha, n_prompt_tiles):
    i = pl.program_id(0)

    @pl.when(i == 0)
    def _():
        carry_ref[...] = jnp.zeros_like(carry_ref)

    is_prompt = i < n_prompt_tiles
    oa = jnp.where(is_prompt, oap_ref[...], oas_ref[...])
    ob = jnp.where(is_prompt, obp_ref[...], obs_ref[...])
    ya = _dot(oa.astype(BF16), wa_ref[...])
    yb = _dot(ob.astype(BF16), wb_ref[...])
    m = _sigmoid(ga_ref[...]) * ya + _sigmoid(gb_ref[...]) * yb
    mix = _dot(m.astype(BF16), wo_ref[...])
    h = _layer_norm(alpha * x_ref[...] + mix, lng_ref[...], lnb_ref[...])
    h_ref[...] = h

    h_hi = h.astype(BF16)
    h_lo = (h - h_hi.astype(F32)).astype(BF16)
    logits = (_dot_nt(h_hi, wrh_ref[...]) + _dot_nt(h_lo, wrh_ref[...]) + _dot_nt(h_hi, wrl_ref[...])
              + br_ref[...])
    tm = logits.shape[0]
    lane = lax.broadcasted_iota(jnp.int32, (tm, LANES), 1)
    work = logits
    idxs, vals = [], []
    for _ in range(TOP_K):
        mx = jnp.max(work, axis=1, keepdims=True)
        ix = jnp.min(jnp.where(work == mx, lane, LANES), axis=1, keepdims=True)
        idxs.append(ix)
        vals.append(mx)
        work = jnp.where(lane == ix, -jnp.inf, work)
    exps = [jnp.exp(v - vals[0]) for v in vals]
    denom = exps[0] + exps[1] + exps[2] + exps[3]
    multihot = jnp.zeros((tm, LANES), F32)
    for ix in idxs:
        multihot = multihot + jnp.where(lane == ix, 1.0, 0.0)
    r = lax.broadcasted_iota(jnp.int32, (tm, tm), 0)
    c = lax.broadcasted_iota(jnp.int32, (tm, tm), 1)
    before = _dot((c < r).astype(BF16), multihot.astype(BF16)) + carry_ref[...]
    carry_ref[...] = carry_ref[...] + jnp.sum(multihot, axis=0, keepdims=True)
    col4 = lax.broadcasted_iota(jnp.int32, (tm, TOP_K), 1)
    eidx = jnp.zeros((tm, TOP_K), jnp.int32)
    gates = jnp.zeros((tm, TOP_K), F32)
    ranks = jnp.zeros((tm, TOP_K), F32)
    for k in range(TOP_K):
        rk = jnp.sum(jnp.where(lane == idxs[k], before, 0.0), axis=1, keepdims=True)
        eidx = jnp.where(col4 == k, idxs[k], eidx)
        gates = jnp.where(col4 == k, exps[k] / denom, gates)
        ranks = jnp.where(col4 == k, rk, ranks)
    eidx_ref[...] = eidx
    gate_ref[...] = gates
    rank_ref[...] = ranks.astype(jnp.int32)
    cnt_ref[...] = carry_ref[...].astype(jnp.int32)


def _mix(oa_p, oa_s, ob_p, ob_s, proj, x2d, wa, wb, wo, lng, lnb, wrh, wrl, br, alpha):
    t = x2d.shape[0]
    t_p, t_s = oa_p.shape[0], oa_s.shape[0]
    tm = next(c for c in (256, 128, 64, 32, 16, 8) if t_p % c == 0 and t_s % c == 0)
    npt = t_p // tm
    row = lambda i: (i, 0)
    prow = lambda i: (jnp.minimum(i, npt - 1), 0)
    srow = lambda i: (jnp.maximum(i - npt, 0), 0)
    const = lambda i: (0, 0)
    wspec = pl.BlockSpec((D_MODEL, D_MODEL), const)
    vspec = pl.BlockSpec((1, D_MODEL), const)
    return pl.pallas_call(
        functools.partial(_mix_kernel, alpha=alpha, n_prompt_tiles=npt),
        grid=(t // tm,),
        in_specs=[pl.BlockSpec((tm, D_MODEL), prow), pl.BlockSpec((tm, D_MODEL), srow),
                  pl.BlockSpec((tm, D_MODEL), prow), pl.BlockSpec((tm, D_MODEL), srow),
                  pl.BlockSpec((tm, D_MODEL), lambda i: (i, PROJ_G0 // D_MODEL)),
                  pl.BlockSpec((tm, D_MODEL), lambda i: (i, PROJ_G0 // D_MODEL + 1)),
                  pl.BlockSpec((tm, D_MODEL), row), wspec, wspec, wspec, vspec, vspec,
                  pl.BlockSpec((LANES, D_MODEL), const), pl.BlockSpec((LANES, D_MODEL), const),
                  pl.BlockSpec((1, LANES), const)],
        out_specs=[pl.BlockSpec((tm, D_MODEL), row), pl.BlockSpec((tm, TOP_K), row),
                   pl.BlockSpec((tm, TOP_K), row), pl.BlockSpec((tm, TOP_K), row),
                   pl.BlockSpec((1, LANES), const)],
        out_shape=[jax.ShapeDtypeStruct((t, D_MODEL), F32), jax.ShapeDtypeStruct((t, TOP_K), jnp.int32),
                   jax.ShapeDtypeStruct((t, TOP_K), F32), jax.ShapeDtypeStruct((t, TOP_K), jnp.int32),
                   jax.ShapeDtypeStruct((1, LANES), jnp.int32)],
        scratch_shapes=[pltpu.VMEM((1, LANES), F32)],
        compiler_params=_cparams(("arbitrary",)),
        name="mix_router",
    )(oa_p, oa_s, ob_p, ob_s, proj, proj, x2d, wa, wb, wo, lng, lnb, wrh, wrl, br)


def _dispatch_kernel(dest_ref, last_ref, has_ref, nused_ref, h_ref, x_hbm, stage, zeros, sems, zsem,
                     *, tt, n_tiles, min_used):
    i = pl.program_id(0)
    n = pl.num_programs(0)
    slot = i % 2
    tm = MOE_TILE

    @pl.when(i == 0)
    def _():
        zeros[...] = jnp.zeros_like(zeros)

        def fill(tile):
            return pltpu.make_async_copy(zeros, x_hbm.at[pl.ds(tile * tm, tm), :], zsem)

        for e in range(N_EXPERTS):
            pl.when(has_ref[e] > 0)(lambda e=e: fill(last_ref[e]).start())
        for j in range(n_tiles - min_used):
            pl.when(nused_ref[0] + j < n_tiles)(lambda j=j: fill(nused_ref[0] + j).start())
        for e in range(N_EXPERTS):
            pl.when(has_ref[e] > 0)(lambda e=e: fill(last_ref[e]).wait())
        for j in range(n_tiles - min_used):
            pl.when(nused_ref[0] + j < n_tiles)(lambda j=j: fill(nused_ref[0] + j).wait())

    stage[slot] = h_ref[...]
    for r in range(tt):
        for k in range(TOP_K):
            row = dest_ref[(i * tt + r) * TOP_K + k]
            pltpu.make_async_copy(stage.at[slot, pl.ds(r, 1), :], x_hbm.at[pl.ds(row, 1), :],
                                  sems.at[slot]).start(priority=k % 2)

    def wait_slot(s):
        for _ in range(TOP_K):
            pltpu.make_async_copy(stage.at[s], x_hbm.at[pl.ds(0, tt), :], sems.at[s]).wait()

    @pl.when(i > 0)
    def _():
        wait_slot(1 - slot)

    @pl.when(i == n - 1)
    def _():
        wait_slot(slot)


def _dispatch(h, dest, last_tile, has_rows, n_used, n_tiles):
    t = h.shape[0]
    tt = _pick_tile(t, (256, 128, 64, 16, 8))
    min_used = (t * TOP_K) // MOE_TILE
    grid_spec = pltpu.PrefetchScalarGridSpec(
        num_scalar_prefetch=4,
        grid=(t // tt,),
        in_specs=[pl.BlockSpec((tt, D_MODEL), lambda i, *_: (i, 0))],
        out_specs=pl.BlockSpec(memory_space=pl.ANY),
        scratch_shapes=[pltpu.VMEM((2, tt, D_MODEL), F32), pltpu.VMEM((MOE_TILE, D_MODEL), F32),
                        pltpu.SemaphoreType.DMA((2,)), pltpu.SemaphoreType.DMA],
    )
    return pl.pallas_call(
        functools.partial(_dispatch_kernel, tt=tt, n_tiles=n_tiles, min_used=min_used),
        grid_spec=grid_spec,
        out_shape=jax.ShapeDtypeStruct((n_tiles * MOE_TILE, D_MODEL), F32),
        compiler_params=_cparams(("arbitrary",)),
        name="moe_dispatch",
    )(dest, last_tile, has_rows, n_used, h)


def _expert_kernel(te_ref, nused_ref, x_ref, wgu_ref, bgu_ref, wd_ref, bd_ref, y_ref, wgu_bf, wd_bf):
    i = pl.program_id(0)
    used = i < nused_ref[0]
    prev = te_ref[jnp.maximum(i - 1, 0)]
    fresh = jnp.logical_or(i == 0, te_ref[i] != prev)

    @pl.when(jnp.logical_and(used, fresh))
    def _():
        wgu_bf[...] = wgu_ref[0].astype(BF16)
        wd_bf[...] = wd_ref[0].astype(BF16)

    @pl.when(used)
    def _():
        x = x_ref[...].astype(BF16)
        hcat = _dot(x, wgu_bf[...]) + bgu_ref[0]
        dff = hcat.shape[1] // 2
        gate = jnp.minimum(hcat[:, :dff], SWIGLU_LIMIT)
        up = jnp.clip(hcat[:, dff:], -SWIGLU_LIMIT, SWIGLU_LIMIT)
        act = (up + 1.0) * gate * _sigmoid(SWIGLU_ALPHA * gate)
        y_ref[...] = _dot(act.astype(BF16), wd_bf[...]) + bd_ref[0]

    @pl.when(jnp.logical_not(used))
    def _():
        y_ref[...] = jnp.zeros_like(y_ref)


def _experts(x_sorted, tile_expert, n_used, wgu, bgu, wd, bd):
    tm = MOE_TILE
    n_tiles = x_sorted.shape[0] // tm
    dff2 = wgu.shape[2]
    grid_spec = pltpu.PrefetchScalarGridSpec(
        num_scalar_prefetch=2,
        grid=(n_tiles,),
        in_specs=[pl.BlockSpec((tm, D_MODEL), lambda i, te, n: (jnp.minimum(i, n[0] - 1), 0)),
                  pl.BlockSpec((1, D_MODEL, dff2), lambda i, te, n: (te[i], 0, 0)),
                  pl.BlockSpec((1, 1, dff2), lambda i, te, n: (te[i], 0, 0)),
                  pl.BlockSpec((1, dff2 // 2, D_MODEL), lambda i, te, n: (te[i], 0, 0)),
                  pl.BlockSpec((1, 1, D_MODEL), lambda i, te, n: (te[i], 0, 0))],
        out_specs=pl.BlockSpec((tm, D_MODEL), lambda i, te, n: (i, 0)),
        scratch_shapes=[pltpu.VMEM((D_MODEL, dff2), BF16), pltpu.VMEM((dff2 // 2, D_MODEL), BF16)],
    )
    return pl.pallas_call(
        _expert_kernel,
        grid_spec=grid_spec,
        out_shape=jax.ShapeDtypeStruct((n_tiles * tm, D_MODEL), F32),
        compiler_params=_cparams(("arbitrary",)),
        name="moe_experts",
    )(tile_expert, n_used, x_sorted, wgu, bgu, wd, bd)


def _combine_kernel(dest_ref, y_hbm, gate_ref, h_ref, lng_ref, lnb_ref, o_ref, buf, sems, *, tt, alpha):
    i = pl.program_id(0)
    n = pl.num_programs(0)
    slot = i % 2

    def issue_tile(tile, s):
        for r in range(tt):
            for k in range(TOP_K):
                row = dest_ref[(tile * tt + r) * TOP_K + k]
                pltpu.make_async_copy(y_hbm.at[pl.ds(row, 1), :], buf.at[s, k, pl.ds(r, 1), :],
                                      sems.at[s]).start(priority=k % 2)

    @pl.when(i == 0)
    def _():
        issue_tile(0, 0)

    @pl.when(i + 1 < n)
    def _():
        issue_tile(i + 1, 1 - slot)

    for k in range(TOP_K):
        pltpu.make_async_copy(y_hbm.at[pl.ds(0, tt), :], buf.at[slot, k], sems.at[slot]).wait()
    gates = gate_ref[...]
    ffn = gates[:, 0:1] * buf[slot, 0]
    for k in range(1, TOP_K):
        ffn = ffn + gates[:, k:k + 1] * buf[slot, k]
    o_ref[...] = _layer_norm(alpha * h_ref[...] + ffn, lng_ref[...], lnb_ref[...])


def _combine(y_sorted, dest, gates, h, lng, lnb, alpha):
    t = h.shape[0]
    tt = _pick_tile(t, (128, 64, 16, 8))
    grid_spec = pltpu.PrefetchScalarGridSpec(
        num_scalar_prefetch=1,
        grid=(t // tt,),
        in_specs=[pl.BlockSpec(memory_space=pl.ANY),
                  pl.BlockSpec((tt, TOP_K), lambda i, d: (i, 0)),
                  pl.BlockSpec((tt, D_MODEL), lambda i, d: (i, 0)),
                  pl.BlockSpec((1, D_MODEL), lambda i, d: (0, 0)),
                  pl.BlockSpec((1, D_MODEL), lambda i, d: (0, 0))],
        out_specs=pl.BlockSpec((tt, D_MODEL), lambda i, d: (i, 0)),
        scratch_shapes=[pltpu.VMEM((2, TOP_K, tt, D_MODEL), F32), pltpu.SemaphoreType.DMA((2,))],
    )
    return pl.pallas_call(
        functools.partial(_combine_kernel, tt=tt, alpha=alpha),
        grid_spec=grid_spec,
        out_shape=jax.ShapeDtypeStruct((t, D_MODEL), F32),
        compiler_params=_cparams(("arbitrary",)),
        name="moe_combine",
    )(dest, y_sorted, gates, h, lng, lnb)


def _moe_plan(eidx, rank, counts, n_tiles, expert0):
    tm = MOE_TILE
    cnt = counts[0, :N_EXPERTS]
    padded = (cnt + tm - 1) // tm * tm
    ends = jnp.cumsum(padded)
    starts = ends - padded
    onehot = eidx[..., None] == jnp.arange(N_EXPERTS, dtype=eidx.dtype)
    dest = (jnp.sum(jnp.where(onehot, starts.astype(jnp.int32), 0), axis=-1) + rank).reshape(-1).astype(jnp.int32)
    tile_start = jnp.arange(n_tiles, dtype=jnp.int32) * tm
    tile_expert = jnp.sum((tile_start[:, None] >= ends[None, :]).astype(jnp.int32), axis=1)
    tile_expert = (jnp.minimum(tile_expert, N_EXPERTS - 1) + expert0).astype(jnp.int32)
    n_used = (ends[-1] // tm).astype(jnp.int32).reshape(1)
    last_tile = (ends // tm - 1).astype(jnp.int32)
    has_rows = (padded > 0).astype(jnp.int32)
    return dest, tile_expert, n_used, last_tile, has_rows


def _layer(x2d, lw, shared, layer, depth, dims, alpha, kv_prev):
    nb_p, seq_p, nb_s, seq_s = dims
    t_p = nb_p * seq_p
    t_s = nb_s * seq_s
    page_table = shared["page_table"]
    page0 = layer * shared["n_pool"]
    proj = _project(x2d, lw["wt_main"])
    kbuf, vbuf = _kv_prompt(x2d, lw["wt_kv"], layer, depth, nb_p, seq_p, kv_prev)
    kv_s = _project(x2d, lw["wt_kv"], row0=t_p, rows=t_s)
    logft_p, cum_p, cumt_p = _fox_gate_prompt(x2d, lw["wt_f"], lw["b_f"], nb_p, seq_p)
    oa_p, s_p = _hgrn(proj, lw["lb"], lw["hgrn_g"], None, 0, 0, nb_p, seq_p)
    oa_s, s_s = _hgrn(proj, lw["lb"], lw["hgrn_g"], shared["state"], layer * nb_s, t_p, nb_s, seq_s)
    ob_p = _fox_prompt(proj, kbuf, vbuf, layer, cum_p, cumt_p, nb_p, seq_p)
    ob_s, logft_s = _fox_sample(proj, kv_s, x2d, lw["wt_f"], lw["b_f"], shared["cache_kt"], shared["cache_vt"],
                                shared["cache_logf_t"], page_table, page0, t_p, nb_s, seq_s)
    h, eidx, gates, rank, counts = _mix(oa_p, oa_s, ob_p, ob_s, proj, x2d, lw["w_a"], lw["w_b"], lw["w_o"],
                                        lw["ln_mix_g"], lw["ln_mix_b"], lw["wr_hi"], lw["wr_lo"], lw["b_r"], alpha)
    t = x2d.shape[0]
    n_tiles = (t * TOP_K + N_EXPERTS * (MOE_TILE - 1)) // MOE_TILE + 1
    dest, tile_expert, n_used, last_tile, has_rows = _moe_plan(eidx, rank, counts, n_tiles, layer * N_EXPERTS)
    x_sorted = _dispatch(h, dest, last_tile, has_rows, n_used, n_tiles)
    y_sorted = _experts(x_sorted, tile_expert, n_used,
                        shared["w_gate_up"], shared["b_gate_up"], shared["w_down"], shared["b_down"])
    y = _combine(y_sorted, dest, gates, h, lw["ln_ffn_g"], lw["ln_ffn_b"], alpha)
    return y, (kbuf, vbuf), kv_s, logft_p, logft_s, s_p, s_s


def kernel(x_prompt, x_sample, cache_k, cache_v, cache_logf, state_hgrn, page_table, w_in, b_fox_f, lb_logits,
           hgrn_norm_g, w_branch_a, w_branch_b, w_out, ln_mix_g, ln_mix_b, w_router, b_router, w_gate_up,
           b_gate_up, w_down, b_down, ln_ffn_g, ln_ffn_b):
    depth = w_in.shape[0]
    nb_p, seq_p, d = x_prompt.shape
    nb_s, seq_s, _ = x_sample.shape
    t_p, t_s = nb_p * seq_p, nb_s * seq_s
    alpha = (2 * depth) ** 0.25
    n_pool, page = cache_k.shape[1], cache_k.shape[2]
    dff2 = w_gate_up.shape[-1]

    pl_soft = jax.nn.softmax(lb_logits.astype(F32), axis=0)
    lower_bounds = jnp.cumsum(pl_soft, axis=0) - pl_soft[0:1]

    x2d = jnp.concatenate([x_prompt.reshape(t_p, d), x_sample.reshape(t_s, d)], axis=0)
    shared = {
        "page_table": page_table, "n_pool": n_pool,
        "cache_kt": cache_k.transpose(0, 1, 3, 4, 2).reshape(depth * n_pool, D_MODEL, page),
        "cache_vt": cache_v.transpose(0, 1, 3, 4, 2).reshape(depth * n_pool, D_MODEL, page),
        "cache_logf_t": cache_logf.transpose(0, 1, 3, 2).reshape(depth * n_pool, B_HEADS, page),
        "state": state_hgrn.reshape(depth * nb_s, A_HEADS, A_DK, A_DK),
        "w_gate_up": w_gate_up.reshape(depth * N_EXPERTS, d, dff2),
        "b_gate_up": b_gate_up.reshape(depth * N_EXPERTS, 1, dff2),
        "w_down": w_down.reshape(depth * N_EXPERTS, dff2 // 2, d),
        "b_down": b_down.reshape(depth * N_EXPERTS, 1, d),
    }
    wt_in = w_in.transpose(0, 2, 1)
    c_k = 5 * D_MODEL
    c_f = 7 * D_MODEL
    pad_r = LANES - N_EXPERTS
    kv_prev = None
    fp, sp, ks, vs, fs, ss = [], [], [], [], [], []
    for l in range(depth):
        wt = wt_in[l]
        wr_t = jnp.pad(w_router[l].T, ((0, pad_r), (0, 0)))
        wr_hi = wr_t.astype(BF16)
        lw = {
            "wt_main": jnp.concatenate([wt[:c_k], wt[c_f + B_HEADS:]], axis=0).astype(BF16),
            "wt_kv": wt[c_k:c_f].astype(BF16),
            "wt_f": wt[c_f:c_f + B_HEADS].astype(BF16),
            "b_f": b_fox_f[l],
            "lb": lower_bounds[l].reshape(1, d),
            "hgrn_g": hgrn_norm_g[l].reshape(1, A_DK),
            "w_a": w_branch_a[l].astype(BF16), "w_b": w_branch_b[l].astype(BF16), "w_o": w_out[l].astype(BF16),
            "ln_mix_g": ln_mix_g[l].reshape(1, d), "ln_mix_b": ln_mix_b[l].reshape(1, d),
            "ln_ffn_g": ln_ffn_g[l].reshape(1, d), "ln_ffn_b": ln_ffn_b[l].reshape(1, d),
            "b_r": jnp.pad(b_router[l], (0, pad_r), constant_values=MASK_VALUE).reshape(1, LANES),
            "wr_hi": wr_hi, "wr_lo": (wr_t - wr_hi.astype(F32)).astype(BF16),
        }
        x2d, kv_prev, kv_s, logft_p, logft_s, s_p, s_s = _layer(
            x2d, lw, shared, l, depth, (nb_p, seq_p, nb_s, seq_s), alpha, kv_prev)
        ks.append(kv_s[:, :D_MODEL].reshape(nb_s, seq_s, B_HEADS, B_DH))
        vs.append(kv_s[:, D_MODEL:].reshape(nb_s, seq_s, B_HEADS, B_DH))
        fp.append(logft_p)
        fs.append(logft_s)
        sp.append(s_p)
        ss.append(s_s)
    y_p = x2d[:t_p].reshape(nb_p, seq_p, d)
    y_s = x2d[t_p:].reshape(nb_s, seq_s, d)
    kbuf, vbuf = kv_prev
    k_p = kbuf.reshape(depth, nb_p, B_HEADS, B_DH, seq_p).transpose(0, 1, 4, 2, 3)
    v_p = vbuf.reshape(depth, nb_p, B_HEADS, B_DH, seq_p).transpose(0, 1, 4, 2, 3)
    f_p = jnp.stack(fp).transpose(0, 1, 3, 2)
    f_s = jnp.stack(fs).transpose(0, 1, 3, 2)
    return (y_p, y_s, k_p, v_p, f_p, jnp.stack(sp), jnp.stack(ks), jnp.stack(vs), f_s, jnp.stack(ss))
```
